```python
import jax, jax.numpy as jnp
from jax import lax
import numpy as np

D_MODEL = 1024
BATCH = 16
SEQ = 2048
DEPTH = 4

CTX_LEN = 256
GRID_W = 64
EXPAND = 2
MIX_WIDTH = EXPAND * D_MODEL
HGRN_WIDTH = MIX_WIDTH // 2
FOURIER_WIDTH = MIX_WIDTH - HGRN_WIDTH
HGRN_HEAD_DIM = 128
HGRN_HEADS = HGRN_WIDTH // HGRN_HEAD_DIM
FOURIER_GROUPS = 4
FOURIER_GROUP_DIM = FOURIER_WIDTH // FOURIER_GROUPS
IN_WIDTH = 5 * HGRN_WIDTH + 2 * FOURIER_WIDTH
CHUNK = 32
POS_BASE = 10000.0
EPS = 1e-6
F_MIN = 1e-6

kernel_name = "hybrid_hgrn2_fnet_prefix"

F32 = jnp.float32


def rmsnorm(x, w):
    xf = x.astype(F32)
    y = xf * lax.rsqrt(jnp.mean(xf * xf, axis=-1, keepdims=True) + EPS)
    return (y * w.astype(F32)).astype(x.dtype)


def sincos_2d(length, dim):
    rows = length // GRID_W
    row = jnp.repeat(jnp.arange(rows, dtype=F32), GRID_W)
    col = jnp.tile(jnp.arange(GRID_W, dtype=F32), rows)
    quarter = dim // 4
    omega = jnp.power(POS_BASE, -jnp.arange(quarter, dtype=F32) / quarter)

    def axis_code(p):
        ang = p[:, None] * omega[None, :]
        return jnp.concatenate([jnp.sin(ang), jnp.cos(ang)], axis=-1)

    return jnp.concatenate([axis_code(row), axis_code(col)], axis=-1)


def to_heads(t):
    return t.reshape(t.shape[:-1] + (HGRN_HEADS, HGRN_HEAD_DIM))


def split_proj(p):
    a = HGRN_WIDTH
    return jnp.split(p, [a, 2 * a, 3 * a, 4 * a, 5 * a, 5 * a + FOURIER_WIDTH], axis=-1)


def layer_lower_bounds(lower_bounds):
    p = jax.nn.softmax(lower_bounds.astype(F32), axis=1)
    return jnp.cumsum(p, axis=1) - p[:, :1]


def forget_gate(z, lb):
    z = z.astype(F32)
    lb = lb.reshape(HGRN_HEADS, HGRN_HEAD_DIM)
    k = (1.0 - lb) * jax.nn.sigmoid(-z)
    log_f = jnp.log(jnp.maximum(1.0 - k, F_MIN))
    return k, log_f


def chunk_scan(q, k, v, log_f, s0):
    b_, length, h, _ = q.shape
    dv = v.shape[-1]
    n = length // CHUNK

    def blocks(t):
        return t.reshape(b_, n, CHUNK, h, t.shape[-1]).transpose(1, 0, 3, 2, 4)

    tri = jnp.tril(jnp.ones((CHUNK, CHUNK), dtype=bool))[:, :, None]

    def step(state, inp):
        qc, kc, vc, gc = inp
        bcum = jnp.cumsum(gc, axis=2)
        o_inter = jnp.einsum("bhtd,bhde->bhte", qc * jnp.exp(bcum), state)
        rel = jnp.where(tri, bcum[:, :, :, None, :] - bcum[:, :, None, :, :], 0.0)
        decay = jnp.where(tri, jnp.exp(rel), 0.0)
        scores = jnp.einsum("bhtd,bhsd,bhtsd->bhts", qc, kc, decay)
        o_intra = jnp.einsum("bhts,bhse->bhte", scores, vc)
        b_end = bcum[:, :, -1:, :]
        state = state * jnp.exp(b_end)[:, :, 0, :, None] + jnp.einsum(
            "bhsd,bhse->bhde", kc * jnp.exp(b_end - bcum), vc)
        return state, o_inter + o_intra

    s_final, o = lax.scan(step, s0, (blocks(q), blocks(k), blocks(v), blocks(log_f)))
    return o.transpose(1, 0, 3, 2, 4).reshape(b_, length, h, dv), s_final


def final_state(k, v, log_f):
    bcum = jnp.cumsum(log_f, axis=1)
    w = k * jnp.exp(bcum[:, -1:] - bcum)
    return jnp.einsum("bthd,bthe->bhde", w, v)


def fourier_mix(u, w_four, b_four):
    b_, length, _ = u.shape
    ug = u.astype(F32).reshape(b_, length, FOURIER_GROUPS, FOURIER_GROUP_DIM)
    spec = jnp.fft.fftn(ug, axes=(1, 3), norm="ortho").real
    y = jnp.einsum("blgc,gcd->blgd", spec, w_four.astype(F32)).reshape(b_, length, FOURIER_WIDTH)
    return (y + b_four.astype(F32)).astype(u.dtype)


def mix_stream(h, w_in, lb_f, lb_b, s0_f, s0_b, hn_w, w_four, b_four, w_out):
    p = h @ w_in
    i_in, z_f, z_b, q_in, gate_a, u_f, gate_f = split_proj(p)
    v = to_heads(i_in).astype(F32)
    q = jax.nn.silu(to_heads(q_in).astype(F32))
    k_f, g_f = forget_gate(to_heads(z_f), lb_f)
    k_b, g_b = forget_gate(to_heads(z_b), lb_b)
    o_f, s_f = chunk_scan(q, k_f, v, g_f, s0_f)
    flip = lambda t: jnp.flip(t, axis=1)
    o_b, s_b = chunk_scan(flip(q), flip(k_b), flip(v), flip(g_b), s0_b)
    o = o_f + flip(o_b)
    o = o * lax.rsqrt(jnp.mean(o * o, axis=-1, keepdims=True) + EPS)
    o = o * hn_w.astype(F32).reshape(HGRN_HEADS, HGRN_HEAD_DIM)
    y_a = o.reshape(h.shape[:-1] + (HGRN_WIDTH,)).astype(h.dtype) * jax.nn.silu(gate_a)
    y_f = fourier_mix(u_f, w_four, b_four) * jax.nn.silu(gate_f)
    out = jnp.concatenate([y_a, y_f], axis=-1) @ w_out
    return out, s_f, s_b


def context_final_states(h, w_in, lb_f, lb_b):
    p = h @ w_in[:, :3 * HGRN_WIDTH]
    i_in, z_f, z_b = jnp.split(p, [HGRN_WIDTH, 2 * HGRN_WIDTH], axis=-1)
    v = to_heads(i_in).astype(F32)
    k_f, g_f = forget_gate(to_heads(z_f), lb_f)
    k_b, g_b = forget_gate(to_heads(z_b), lb_b)
    flip = lambda t: jnp.flip(t, axis=1)
    return final_state(k_f, v, g_f), final_state(flip(k_b), flip(v), flip(g_b))


def setup_inputs(seed: int = 0) -> dict:
    key = jax.random.key(seed)
    ks = jax.random.split(key, 14)
    d = D_MODEL
    nrm = lambda k, s: jax.random.normal(k, s, dtype=F32)
    return {
        "x": nrm(ks[0], (BATCH, SEQ, d)),
        "c": nrm(ks[1], (BATCH, d)),
        "ctx": nrm(ks[2], (BATCH, CTX_LEN, d)),
        "c_ctx": nrm(ks[3], (d,)),
        "norm_w": 1.0 + 0.02 * nrm(ks[4], (DEPTH, d)),
        "w_ada": 0.5 * d ** -0.5 * nrm(ks[5], (DEPTH, d, 3 * d)),
        "b_ada": 0.01 * nrm(ks[6], (DEPTH, 3 * d)),
        "w_in": d ** -0.5 * nrm(ks[7], (DEPTH, d, IN_WIDTH)),
        "lower_bounds": 1.0 + 0.1 * nrm(ks[8], (2, DEPTH, HGRN_WIDTH)),
        "hgrn_norm_w": 1.0 + 0.02 * nrm(ks[9], (DEPTH, HGRN_WIDTH)),
        "w_fourier": FOURIER_GROUP_DIM ** -0.5 * nrm(ks[10], (DEPTH, FOURIER_GROUPS, FOURIER_GROUP_DIM, FOURIER_GROUP_DIM)),
        "b_fourier": 0.01 * nrm(ks[11], (DEPTH, FOURIER_WIDTH)),
        "w_out": MIX_WIDTH ** -0.5 * nrm(ks[12], (DEPTH, MIX_WIDTH, d)),
        "final_norm_w": 1.0 + 0.02 * nrm(ks[13], (d,)),
    }


def reference(x, c, ctx, c_ctx, norm_w, w_ada, b_ada, w_in, lower_bounds, hgrn_norm_w,
              w_fourier, b_fourier, w_out, final_norm_w):
    b_, length, d = x.shape
    x = x + sincos_2d(length, d).astype(x.dtype)[None]
    lbs = layer_lower_bounds(lower_bounds)
    sc = jax.nn.silu(c)
    scc = jax.nn.silu(c_ctx)
    zero = jnp.zeros((b_, HGRN_HEADS, HGRN_HEAD_DIM, HGRN_HEAD_DIM), dtype=F32)
    for l in range(DEPTH):
        shift, scale, gate = jnp.split(sc @ w_ada[l] + b_ada[l], 3, axis=-1)
        shift_c, scale_c, gate_c = jnp.split(scc @ w_ada[l] + b_ada[l], 3, axis=-1)
        hc = rmsnorm(ctx, norm_w[l]) * (1.0 + scale_c) + shift_c
        if l < DEPTH - 1:
            out_c, s_f, s_b = mix_stream(hc, w_in[l], lbs[0, l], lbs[1, l], zero, zero,
                                         hgrn_norm_w[l], w_fourier[l], b_fourier[l], w_out[l])
        else:
            s_f, s_b = context_final_states(hc, w_in[l], lbs[0, l], lbs[1, l])
        hx = rmsnorm(x, norm_w[l]) * (1.0 + scale[:, None]) + shift[:, None]
        out_x, _, _ = mix_stream(hx, w_in[l], lbs[0, l], lbs[1, l], s_f, s_b,
                                 hgrn_norm_w[l], w_fourier[l], b_fourier[l], w_out[l])
        x = x + gate[:, None] * out_x
        if l < DEPTH - 1:
            ctx = ctx + gate_c * out_c
    return rmsnorm(x, final_norm_w)
```

```python
import functools
import math

import jax
import jax.numpy as jnp
from jax import lax
from jax.experimental import pallas as pl
from jax.experimental.pallas import tpu as pltpu

F32 = jnp.float32
BF16 = jnp.bfloat16

LANES = 128
HEAD_DIM = 128
GROUP_DIM = 256
FOURIER_GROUPS = 4
GRID_W = 64
POS_BASE = 10000.0
EPS = 1e-6
F_MIN = 1e-6
SCAN_CHUNK = 128
BASE_BLOCK = 8
VMEM_LIMIT = 56 * 1024 * 1024


def _sigmoid(z):
    return 1.0 / (1.0 + jnp.exp(-z))


def _silu(z):
    return z * _sigmoid(z)


def _dot(a, b):
    return jnp.dot(a, b, preferred_element_type=F32)


def _dot_nt(a, b):
    return lax.dot_general(a, b, (((1,), (1,)), ((), ())), preferred_element_type=F32)


def _dot_tn(a, b):
    return lax.dot_general(a, b, (((0,), (0,)), ((), ())), preferred_element_type=F32)


def _ada_kernel(c_ref, w_ref, b_ref, o_ref):
    sc = _silu(c_ref[...]).astype(BF16)
    o_ref[0] = _dot(sc, w_ref[0].astype(BF16)) + b_ref[0]


def _ada_all(c_rows, w_ada, b_ada):
    depth, d, n = w_ada.shape
    rows = c_rows.shape[0]
    tn = 1024
    return pl.pallas_call(
        _ada_kernel,
        grid=(depth, n // tn),
        in_specs=[
            pl.BlockSpec((rows, d), lambda l, j: (0, 0)),
            pl.BlockSpec((1, d, tn), lambda l, j: (l, 0, j)),
            pl.BlockSpec((1, 1, tn), lambda l, j: (l, 0, j)),
        ],
        out_specs=pl.BlockSpec((1, rows, tn), lambda l, j: (l, 0, j)),
        out_shape=jax.ShapeDtypeStruct((depth, rows, n), F32),
        name="ada_mod",
    )(c_rows, w_ada, b_ada.reshape(depth, 1, n))


def _lower_bound_kernel(depth, lb_ref, o_ref):
    for dr in range(2):
        rows = [lb_ref[dr * depth + l:dr * depth + l + 1, :] for l in range(depth)]
        mx = functools.reduce(jnp.maximum, rows)
        ex = [jnp.exp(r - mx) for r in rows]
        den = functools.reduce(lambda a, b: a + b, ex)
        p = [e / den for e in ex]
        run = p[0]
        for l in range(depth):
            if l > 0:
                run = run + p[l]
            o_ref[dr * depth + l:dr * depth + l + 1, :] = run - p[0]


def _lower_bounds(lower_bounds):
    two, depth, width = lower_bounds.shape
    out = pl.pallas_call(
        functools.partial(_lower_bound_kernel, depth),
        out_shape=jax.ShapeDtypeStruct((two * depth, width), F32),
        name="lower_bounds",
    )(lower_bounds.reshape(two * depth, width))
    return out.reshape(two, depth, width)


def _fold_kernel(cs_ref, w_ref, o_ref):
    o_ref[0, 0] = jnp.dot(cs_ref[...], w_ref[0, 0], preferred_element_type=F32,
                          precision=lax.Precision.HIGHEST).astype(BF16)


def _fold_fourier_weights(w_fourier):
    depth, groups, cg, _ = w_fourier.shape
    cs = jnp.concatenate(_dft_tables(cg), axis=0)
    return pl.pallas_call(
        _fold_kernel,
        grid=(depth, groups),
        in_specs=[
            pl.BlockSpec((2 * cg, cg), lambda l, g: (0, 0)),
            pl.BlockSpec((1, 1, cg, cg), lambda l, g: (l, g, 0, 0)),
        ],
        out_specs=pl.BlockSpec((1, 1, 2 * cg, cg), lambda l, g: (l, g, 0, 0)),
        out_shape=jax.ShapeDtypeStruct((depth, groups, 2 * cg, cg), BF16),
        name="fold_fourier",
    )(cs, w_fourier)


def _dft_tables(n):
    j = lax.broadcasted_iota(jnp.int32, (n, n), 0)
    k = lax.broadcasted_iota(jnp.int32, (n, n), 1)
    ang = ((j * k) % n).astype(F32) * (2.0 * math.pi / n)
    return jnp.cos(ang), jnp.sin(ang)


def _sincos_2d(length, dim):
    rows = length // GRID_W
    row = jnp.repeat(jnp.arange(rows, dtype=F32), GRID_W)
    col = jnp.tile(jnp.arange(GRID_W, dtype=F32), rows)
    quarter = dim // 4
    omega = jnp.power(POS_BASE, -jnp.arange(quarter, dtype=F32) / quarter)

    def axis_code(p):
        ang = p[:, None] * omega[None, :]
        return jnp.concatenate([jnp.sin(ang), jnp.cos(ang)], axis=-1)

    return jnp.concatenate([axis_code(row), axis_code(col)], axis=-1)


def _inproj_kernel(add_pos, n_tile, *refs):
    if add_pos:
        x_ref, pos_ref, nw_ref, sc_ref, sh_ref, w_ref, o_ref = refs
    else:
        x_ref, nw_ref, sc_ref, sh_ref, w_ref, o_ref = refs
    x = x_ref[0]
    if add_pos:
        x = x + pos_ref[...]
    ms = jnp.mean(x * x, axis=-1, keepdims=True)
    y = x * lax.rsqrt(ms + EPS) * nw_ref[...]
    h = (y * (1.0 + sc_ref[0]) + sh_ref[0]).astype(BF16)
    n_total = w_ref.shape[1]
    for j in range(n_total // n_tile):
        acc = _dot(h, w_ref[:, j * n_tile:(j + 1) * n_tile])
        for s in range(n_tile // LANES):
            o_ref[0, j * (n_tile // LANES) + s] = acc[:, s * LANES:(s + 1) * LANES].astype(BF16)


def _inproj(x, pos, norm_w, scale, shift, w_bf16):
    b, length, d = x.shape
    n = w_bf16.shape[1]
    tm = min(512, length)
    add_pos = pos is not None
    in_specs = [pl.BlockSpec((1, tm, d), lambda bi, i: (bi, i, 0))]
    args = [x]
    if add_pos:
        in_specs.append(pl.BlockSpec((tm, d), lambda bi, i: (i, 0)))
        args.append(pos)
    in_specs += [
        pl.BlockSpec((1, d), lambda bi, i: (0, 0)),
        pl.BlockSpec((1, 1, d), lambda bi, i: (bi, 0, 0)),
        pl.BlockSpec((1, 1, d), lambda bi, i: (bi, 0, 0)),
        pl.BlockSpec((d, n), lambda bi, i: (0, 0), pipeline_mode=pl.Buffered(1)),
    ]
    args += [norm_w.reshape(1, d), scale.reshape(b, 1, d), shift.reshape(b, 1, d), w_bf16]
    return pl.pallas_call(
        functools.partial(_inproj_kernel, add_pos, 1024),
        grid=(b, length // tm),
        in_specs=in_specs,
        out_specs=pl.BlockSpec((1, n // LANES, tm, LANES), lambda bi, i: (bi, 0, i, 0)),
        out_shape=jax.ShapeDtypeStruct((b, n // LANES, length, LANES), BF16),
        compiler_params=pltpu.CompilerParams(
            dimension_semantics=("parallel", "parallel"), vmem_limit_bytes=VMEM_LIMIT),
        name="inproj",
    )(*args)


def _split_hi_lo(g):
    hi = g.astype(BF16)
    lo = (g - hi.astype(F32)).astype(BF16)
    return jnp.concatenate([hi, lo], axis=1)


def _chunk_cumsum(tri, g):
    out = _dot(tri, _split_hi_lo(g))
    return out[:, :HEAD_DIM] + out[:, HEAD_DIM:]


def _forget(z, one_minus_lb):
    k = one_minus_lb * (1.0 / (1.0 + jnp.exp(z)))
    return k, jnp.log(jnp.maximum(1.0 - k, F_MIN))


def _level_scores(q, k, bc, forward):
    c = q.shape[0]
    out = {}
    w = c // 2
    while w >= BASE_BLOCK:
        nodes = c // (2 * w)
        bc3 = bc.reshape(nodes, 2 * w, HEAD_DIM)
        ref = bc3[:, w - 1:w, :] if forward else bc3[:, w:w + 1, :]
        e = bc3 - ref
        eq = jnp.minimum(e, 0.0)
        ek = jnp.minimum(-e, 0.0)
        qt = (q.reshape(nodes, 2 * w, HEAD_DIM) * jnp.exp(eq)).reshape(c, HEAD_DIM).astype(BF16)
        kt = (k.reshape(nodes, 2 * w, HEAD_DIM) * jnp.exp(ek)).reshape(c, HEAD_DIM).astype(BF16)
        out[w] = _dot_nt(qt, kt)
        w //= 2
    blocks = c // BASE_BLOCK
    bc3 = bc.reshape(blocks, BASE_BLOCK, HEAD_DIM)
    mid = BASE_BLOCK // 2
    e = bc3 - bc3[:, mid - 1:mid, :]
    qt = (q.reshape(blocks, BASE_BLOCK, HEAD_DIM) * jnp.exp(e)).reshape(c, HEAD_DIM).astype(BF16)
    kt = (k.reshape(blocks, BASE_BLOCK, HEAD_DIM) * jnp.exp(-e)).reshape(c, HEAD_DIM).astype(BF16)
    out[0] = _dot_nt(qt, kt)
    return out


def _scan_kernel(n_chunks, v_ref, zf_ref, zb_ref, q_ref, ga_ref, lbf_ref, lbb_ref, hn_ref,
                 s0f_ref, s0b_ref, ya_ref, sf_ref, sb_ref, kb_s, bcb_s, sbin_s, code_s):
    c = SCAN_CHUNK
    t_idx = lax.broadcasted_iota(jnp.int32, (c, c), 0)
    s_idx = lax.broadcasted_iota(jnp.int32, (c, c), 1)
    tri_lo = (s_idx <= t_idx).astype(BF16)
    tri_up = (s_idx >= t_idx).astype(BF16)
    x = t_idx ^ s_idx
    lvl = jnp.ones((c, c), jnp.int32)
    w = BASE_BLOCK
    while w < c:
        lvl = lvl + (x >= w).astype(jnp.int32)
        w *= 2
    code_s[...] = jnp.where(t_idx > s_idx, lvl, jnp.where(t_idx < s_idx, -lvl, 0))

    one_m_lbf = 1.0 - lbf_ref[0]
    one_m_lbb = 1.0 - lbb_ref[0]
    hn = hn_ref[0]

    def backward_body(it, sb):
        i = n_chunks - 1 - it
        rows = pl.ds(pl.multiple_of(i * c, c), c)
        kb, gb = _forget(zb_ref[0, 0, rows, :].astype(F32), one_m_lbb)
        bcb = _chunk_cumsum(tri_up, gb)
        kb_s[rows, :] = kb
        bcb_s[rows, :] = bcb
        sbin_s[i] = sb
        btot = bcb[0:1, :]
        khat = (kb * jnp.exp(btot - bcb)).astype(BF16)
        return sb * jnp.exp(btot) + _dot_tn(v_ref[0, 0, rows, :], khat)

    sb_final = lax.fori_loop(0, n_chunks, backward_body, s0b_ref[0, 0])
    sb_ref[0, 0] = sb_final

    def forward_body(i, sf):
        rows = pl.ds(pl.multiple_of(i * c, c), c)
        kf, gf = _forget(zf_ref[0, 0, rows, :].astype(F32), one_m_lbf)
        bcf = _chunk_cumsum(tri_lo, gf)
        q = _silu(q_ref[0, 0, rows, :].astype(F32))
        v = v_ref[0, 0, rows, :]
        kb = kb_s[rows, :]
        bcb = bcb_s[rows, :]
        rf = _level_scores(q, kf, bcf, True)
        rb = _level_scores(q, kb, bcb, False)
        code = code_s[...]
        p = jnp.where(code == 0, rf[0] + rb[0], 0.0)
        p = jnp.where(code == 1, rf[0], p)
        p = jnp.where(code == -1, rb[0], p)
        w = BASE_BLOCK
        level = 2
        while w < c:
            p = jnp.where(code == level, rf[w], p)
            p = jnp.where(code == -level, rb[w], p)
            w *= 2
            level += 1
        o = _dot(p.astype(BF16), v)
        o = o + _dot_nt((q * jnp.exp(bcf)).astype(BF16), sf.astype(BF16))
        o = o + _dot_nt((q * jnp.exp(bcb)).astype(BF16), sbin_s[i].astype(BF16))
        ftot = bcf[c - 1:c, :]
        khat = (kf * jnp.exp(ftot - bcf)).astype(BF16)
        sf_new = sf * jnp.exp(ftot) + _dot_tn(v, khat)
        ms = jnp.mean(o * o, axis=-1, keepdims=True)
        y = o * lax.rsqrt(ms + EPS) * hn
        ya_ref[0, rows, :] = (y * _silu(ga_ref[0, 0, rows, :].astype(F32))).astype(BF16)
        return sf_new

    sf_ref[0, 0] = lax.fori_loop(0, n_chunks, forward_body, s0f_ref[0, 0])


def _scan(p, lb_f, lb_b, hn_w, s0f_t, s0b_t):
    b, _, length, _ = p.shape
    heads = lb_f.shape[0] // HEAD_DIM
    c = SCAN_CHUNK
    n_chunks = length // c

    def slab(k):
        return pl.BlockSpec((1, 1, length, HEAD_DIM), lambda bi, h: (bi, h + k * heads, 0, 0))

    vec = pl.BlockSpec((1, 1, HEAD_DIM), lambda bi, h: (h, 0, 0))
    state = pl.BlockSpec((1, 1, HEAD_DIM, HEAD_DIM), lambda bi, h: (bi, h, 0, 0))
    return pl.pallas_call(
        functools.partial(_scan_kernel, n_chunks),
        grid=(b, heads),
        in_specs=[slab(0), slab(1), slab(2), slab(3), slab(4), vec, vec, vec, state, state],
        out_specs=[
            pl.BlockSpec((1, length, HEAD_DIM), lambda bi, h: (bi, 0, h)),
            state, state,
        ],
        out_shape=[
            jax.ShapeDtypeStruct((b, length, heads * HEAD_DIM), BF16),
            jax.ShapeDtypeStruct((b, heads, HEAD_DIM, HEAD_DIM), F32),
            jax.ShapeDtypeStruct((b, heads, HEAD_DIM, HEAD_DIM), F32),
        ],
        scratch_shapes=[
            pltpu.VMEM((length, HEAD_DIM), F32),
            pltpu.VMEM((length, HEAD_DIM), F32),
            pltpu.VMEM((n_chunks, HEAD_DIM, HEAD_DIM), F32),
            pltpu.VMEM((c, c), jnp.int32),
        ],
        compiler_params=pltpu.CompilerParams(
            dimension_semantics=("parallel", "parallel"), vmem_limit_bytes=VMEM_LIMIT),
        name="scan",
    )(p, p, p, p, p,
      lb_f.reshape(heads, 1, HEAD_DIM), lb_b.reshape(heads, 1, HEAD_DIM),
      hn_w.reshape(heads, 1, HEAD_DIM), s0f_t, s0b_t)


def _fourier_kernel(length, u_ref, g_ref, ab_ref, bias_ref, cs_ref, y_ref, tt_s):
    cg = GROUP_DIM
    slabs = cg // LANES

    @pl.when(pl.program_id(1) == 0)
    def _():
        rt = min(512, length)
        for g in range(FOURIER_GROUPS):
            for r in range(length // rt):
                rows = slice(r * rt, (r + 1) * rt)
                ug = jnp.concatenate([u_ref[0, slabs * g + s, rows, :] for s in range(slabs)], axis=1)
                tt_s[rows, g * cg:(g + 1) * cg] = _dot(ug, ab_ref[g, :cg, :]).astype(BF16)
                tt_s[length + r * rt:length + (r + 1) * rt, g * cg:(g + 1) * cg] = (
                    _dot(ug, ab_ref[g, cg:, :]).astype(BF16))

    acc = _dot(cs_ref[...], tt_s[...])
    for j in range(acc.shape[1] // LANES):
        cols = slice(j * LANES, (j + 1) * LANES)
        y = (acc[:, cols] + bias_ref[:, cols]) * _silu(g_ref[0, j].astype(F32))
        y_ref[0, :, cols] = y.astype(BF16)


def _fourier(p, u_block, gate_block, ab, bias, cs):
    b, _, length, _ = p.shape
    width = bias.shape[0]
    slabs = width // LANES
    tm = min(512, length)
    return pl.pallas_call(
        functools.partial(_fourier_kernel, length),
        grid=(b, length // tm),
        in_specs=[
            pl.BlockSpec((1, slabs, length, LANES), lambda bi, i: (bi, u_block, 0, 0)),
            pl.BlockSpec((1, slabs, tm, LANES), lambda bi, i: (bi, gate_block, i, 0)),
            pl.BlockSpec(ab.shape, lambda bi, i: (0, 0, 0)),
            pl.BlockSpec((1, width), lambda bi, i: (0, 0)),
            pl.BlockSpec((tm, 2 * length), lambda bi, i: (i, 0)),
        ],
        out_specs=pl.BlockSpec((1, tm, width), lambda bi, i: (bi, i, 0)),
        out_shape=jax.ShapeDtypeStruct((b, length, width), BF16),
        scratch_shapes=[pltpu.VMEM((2 * length, width), BF16)],
        compiler_params=pltpu.CompilerParams(
            dimension_semantics=("parallel", "arbitrary"), vmem_limit_bytes=VMEM_LIMIT),
        name="fourier",
    )(p, p, ab, bias.reshape(1, width), cs)


def _seq_dft_matrix(length):
    cos, sin = _dft_tables(length)
    norm = 1.0 / math.sqrt(length * GROUP_DIM)
    return jnp.concatenate([cos * norm, sin * (-norm)], axis=1).astype(BF16)


def _outproj_kernel(add_pos, final_norm, *refs):
    refs = list(refs)
    ya_ref, yf_ref, w_ref, x_ref, gate_ref = refs[:5]
    rest = refs[5:]
    pos_ref = rest.pop(0) if add_pos else None
    fnw_ref = rest.pop(0) if final_norm else None
    o_ref = rest.pop(0)
    wa = ya_ref.shape[2]
    out = _dot(ya_ref[0], w_ref[:wa, :]) + _dot(yf_ref[0], w_ref[wa:, :])
    x = x_ref[0]
    if add_pos:
        x = x + pos_ref[...]
    x = x + gate_ref[0] * out
    if final_norm:
        ms = jnp.mean(x * x, axis=-1, keepdims=True)
        x = x * lax.rsqrt(ms + EPS) * fnw_ref[...]
    o_ref[0] = x


def _outproj(y_a, y_f, w_bf16, x, gate, pos, final_norm_w):
    b, length, d = x.shape
    wa, wf = y_a.shape[2], y_f.shape[2]
    tm = min(512, length)
    add_pos = pos is not None
    final_norm = final_norm_w is not None
    in_specs = [
        pl.BlockSpec((1, tm, wa), lambda bi, i: (bi, i, 0)),
        pl.BlockSpec((1, tm, wf), lambda bi, i: (bi, i, 0)),
        pl.BlockSpec((wa + wf, d), lambda bi, i: (0, 0)),
        pl.BlockSpec((1, tm, d), lambda bi, i: (bi, i, 0)),
        pl.BlockSpec((1, 1, d), lambda bi, i: (bi, 0, 0)),
    ]
    args = [y_a, y_f, w_bf16, x, gate.reshape(b, 1, d)]
    if add_pos:
        in_specs.append(pl.BlockSpec((tm, d), lambda bi, i: (i, 0)))
        args.append(pos)
    if final_norm:
        in_specs.append(pl.BlockSpec((1, d), lambda bi, i: (0, 0)))
        args.append(final_norm_w.reshape(1, d))
    return pl.pallas_call(
        functools.partial(_outproj_kernel, add_pos, final_norm),
        grid=(b, length // tm),
        in_specs=in_specs,
        out_specs=pl.BlockSpec((1, tm, d), lambda bi, i: (bi, i, 0)),
        out_shape=jax.ShapeDtypeStruct((b, length, d), F32),
        compiler_params=pltpu.CompilerParams(
            dimension_semantics=("parallel", "parallel"), vmem_limit_bytes=VMEM_LIMIT),
        name="outproj",
    )(*args)


def kernel(x, c, ctx, c_ctx, norm_w, w_ada, b_ada, w_in, lower_bounds, hgrn_norm_w,
           w_fourier, b_fourier, w_out, final_norm_w):
    b, length, d = x.shape
    depth = w_in.shape[0]
    hgrn_width = hgrn_norm_w.shape[1]
    heads = hgrn_width // HEAD_DIM
    fourier_width = b_fourier.shape[1]
    u_block = 5 * hgrn_width // fourier_width
    gate_block = u_block + 1
    ctx_len = ctx.shape[1]

    pos = _sincos_2d(length, d)
    lbs = _lower_bounds(lower_bounds)
    rows = ((b + 1 + 7) // 8) * 8
    c_rows = jnp.zeros((rows, d), F32).at[:b].set(c).at[b].set(c_ctx)
    mod = _ada_all(c_rows, w_ada, b_ada)
    ab = _fold_fourier_weights(w_fourier)
    cs_x = _seq_dft_matrix(length)
    cs_c = _seq_dft_matrix(ctx_len)
    w_in_b = w_in.astype(BF16)
    w_out_b = w_out.astype(BF16)
    zero_state = jnp.zeros((b, heads, HEAD_DIM, HEAD_DIM), F32)

    for l in range(depth):
        shift, scale, gate = (mod[l, :b, k * d:(k + 1) * d] for k in range(3))
        shift_c, scale_c, gate_c = (jnp.broadcast_to(mod[l, b:b + 1, k * d:(k + 1) * d], (b, d))
                                    for k in range(3))
        last = l == depth - 1
        pc = _inproj(ctx, None, norm_w[l], scale_c, shift_c, w_in_b[l])
        ya_c, sf_t, sb_t = _scan(pc, lbs[0, l], lbs[1, l], hgrn_norm_w[l], zero_state, zero_state)
        px = _inproj(x, pos if l == 0 else None, norm_w[l], scale, shift, w_in_b[l])
        ya_x, _, _ = _scan(px, lbs[0, l], lbs[1, l], hgrn_norm_w[l], sf_t, sb_t)
        yf_x = _fourier(px, u_block, gate_block, ab[l], b_fourier[l], cs_x)
        x = _outproj(ya_x, yf_x, w_out_b[l], x, gate, pos if l == 0 else None,
                     final_norm_w if last else None)
        if not last:
            yf_c = _fourier(pc, u_block, gate_block, ab[l], b_fourier[l], cs_c)
            ctx = _outproj(ya_c, yf_c, w_out_b[l], ctx, gate_c, None, None)
    return x
```

```python
import functools
import math

import jax
import jax.numpy as jnp
from jax import lax
from jax.experimental import pallas as pl
from jax.experimental.pallas import tpu as pltpu

F32 = jnp.float32
BF16 = jnp.bfloat16

LANES = 128
HEAD_DIM = 128
GROUP_DIM = 256
FOURIER_GROUPS = 4
GRID_W = 64
POS_BASE = 10000.0
EPS = 1e-6
F_MIN = 1e-6
LOG2_E = 1.4426950408889634
SCAN_CHUNK = 128
BASE_BLOCK = 8
CHUNK_GROUP = 4
VMEM_LIMIT = 56 * 1024 * 1024


def _sigmoid(z):
    return 1.0 / (1.0 + jnp.exp(-z))


def _silu(z):
    return z * _sigmoid(z)


def _dot(a, b):
    return jnp.dot(a, b, preferred_element_type=F32)


def _dot_nt(a, b):
    return lax.dot_general(a, b, (((1,), (1,)), ((), ())), preferred_element_type=F32)


def _dot_tn(a, b):
    return lax.dot_general(a, b, (((0,), (0,)), ((), ())), preferred_element_type=F32)


def _ada_kernel(c_ref, w_ref, b_ref, o_ref):
    sc = _silu(c_ref[...]).astype(BF16)
    o_ref[0] = _dot(sc, w_ref[0].astype(BF16)) + b_ref[0]


def _ada_all(c_rows, w_ada, b_ada):
    depth, d, n = w_ada.shape
    rows = c_rows.shape[0]
    tn = 1024
    return pl.pallas_call(
        _ada_kernel,
        grid=(depth, n // tn),
        in_specs=[
            pl.BlockSpec((rows, d), lambda l, j: (0, 0)),
            pl.BlockSpec((1, d, tn), lambda l, j: (l, 0, j)),
            pl.BlockSpec((1, 1, tn), lambda l, j: (l, 0, j)),
        ],
        out_specs=pl.BlockSpec((1, rows, tn), lambda l, j: (l, 0, j)),
        out_shape=jax.ShapeDtypeStruct((depth, rows, n), F32),
        name="ada_mod",
    )(c_rows, w_ada, b_ada.reshape(depth, 1, n))


def _lower_bound_kernel(depth, lb_ref, o_ref):
    for dr in range(2):
        rows = [lb_ref[dr * depth + l:dr * depth + l + 1, :] for l in range(depth)]
        mx = functools.reduce(jnp.maximum, rows)
        ex = [jnp.exp(r - mx) for r in rows]
        den = functools.reduce(lambda a, b: a + b, ex)
        p = [e / den for e in ex]
        run = p[0]
        for l in range(depth):
            if l > 0:
                run = run + p[l]
            o_ref[dr * depth + l:dr * depth + l + 1, :] = run - p[0]


def _lower_bounds(lower_bounds):
    two, depth, width = lower_bounds.shape
    out = pl.pallas_call(
        functools.partial(_lower_bound_kernel, depth),
        out_shape=jax.ShapeDtypeStruct((two * depth, width), F32),
        name="lower_bounds",
    )(lower_bounds.reshape(two * depth, width))
    return out.reshape(two, depth, width)


def _fold_kernel(cs_ref, w_ref, o_ref):
    o_ref[0, 0] = jnp.dot(cs_ref[...], w_ref[0, 0], preferred_element_type=F32,
                          precision=lax.Precision.HIGHEST).astype(BF16)


def _fold_fourier_weights(w_fourier):
    depth, groups, cg, _ = w_fourier.shape
    cs = jnp.concatenate(_dft_tables(cg), axis=0)
    return pl.pallas_call(
        _fold_kernel,
        grid=(depth, groups),
        in_specs=[
            pl.BlockSpec((2 * cg, cg), lambda l, g: (0, 0)),
            pl.BlockSpec((1, 1, cg, cg), lambda l, g: (l, g, 0, 0)),
        ],
        out_specs=pl.BlockSpec((1, 1, 2 * cg, cg), lambda l, g: (l, g, 0, 0)),
        out_shape=jax.ShapeDtypeStruct((depth, groups, 2 * cg, cg), BF16),
        name="fold_fourier",
    )(cs, w_fourier)


def _dft_tables(n):
    j = lax.broadcasted_iota(jnp.int32, (n, n), 0)
    k = lax.broadcasted_iota(jnp.int32, (n, n), 1)
    ang = ((j * k) % n).astype(F32) * (2.0 * math.pi / n)
    return jnp.cos(ang), jnp.sin(ang)


def _sincos_2d(length, dim):
    rows = length // GRID_W
    row = jnp.repeat(jnp.arange(rows, dtype=F32), GRID_W)
    col = jnp.tile(jnp.arange(GRID_W, dtype=F32), rows)
    quarter = dim // 4
    omega = jnp.power(POS_BASE, -jnp.arange(quarter, dtype=F32) / quarter)

    def axis_code(p):
        ang = p[:, None] * omega[None, :]
        return jnp.concatenate([jnp.sin(ang), jnp.cos(ang)], axis=-1)

    return jnp.concatenate([axis_code(row), axis_code(col)], axis=-1)


def _inproj_kernel(add_pos, n_tile, *refs):
    if add_pos:
        x_ref, pos_ref, nw_ref, sc_ref, sh_ref, w_ref, o_ref = refs
    else:
        x_ref, nw_ref, sc_ref, sh_ref, w_ref, o_ref = refs
    x = x_ref[0]
    if add_pos:
        x = x + pos_ref[...]
    ms = jnp.mean(x * x, axis=-1, keepdims=True)
    y = x * lax.rsqrt(ms + EPS) * nw_ref[...]
    h = (y * (1.0 + sc_ref[0]) + sh_ref[0]).astype(BF16)
    n_total = w_ref.shape[1]
    for j in range(n_total // n_tile):
        acc = _dot(h, w_ref[:, j * n_tile:(j + 1) * n_tile])
        for s in range(n_tile // LANES):
            o_ref[0, j * (n_tile // LANES) + s] = acc[:, s * LANES:(s + 1) * LANES].astype(BF16)


def _inproj(x, pos, norm_w, scale, shift, w_bf16):
    b, length, d = x.shape
    n = w_bf16.shape[1]
    tm = min(512, length)
    add_pos = pos is not None
    in_specs = [pl.BlockSpec((1, tm, d), lambda bi, i: (bi, i, 0))]
    args = [x]
    if add_pos:
        in_specs.append(pl.BlockSpec((tm, d), lambda bi, i: (i, 0)))
        args.append(pos)
    in_specs += [
        pl.BlockSpec((1, d), lambda bi, i: (0, 0)),
        pl.BlockSpec((1, 1, d), lambda bi, i: (bi, 0, 0)),
        pl.BlockSpec((1, 1, d), lambda bi, i: (bi, 0, 0)),
        pl.BlockSpec((d, n), lambda bi, i: (0, 0), pipeline_mode=pl.Buffered(1)),
    ]
    args += [norm_w.reshape(1, d), scale.reshape(b, 1, d), shift.reshape(b, 1, d), w_bf16]
    return pl.pallas_call(
        functools.partial(_inproj_kernel, add_pos, 1024),
        grid=(b, length // tm),
        in_specs=in_specs,
        out_specs=pl.BlockSpec((1, n // LANES, tm, LANES), lambda bi, i: (bi, 0, i, 0)),
        out_shape=jax.ShapeDtypeStruct((b, n // LANES, length, LANES), BF16),
        compiler_params=pltpu.CompilerParams(
            dimension_semantics=("parallel", "parallel"), vmem_limit_bytes=VMEM_LIMIT),
        name="inproj",
    )(*args)


def _split_hi_lo(g):
    hi = g.astype(BF16)
    lo = (g - hi.astype(F32)).astype(BF16)
    return jnp.concatenate([hi, lo], axis=1)


def _chunk_cumsum(tri, g):
    out = _dot(tri, _split_hi_lo(g))
    return out[:, :HEAD_DIM] + out[:, HEAD_DIM:]


def _forget(z, one_minus_lb):
    k = one_minus_lb * (1.0 / (1.0 + jnp.exp(z)))
    return k, jnp.log(jnp.maximum(1.0 - k, F_MIN)) * LOG2_E


def _half_level(q, k, bc, w, forward):
    c = q.shape[0]
    qs, ks = [], []
    zeros = jnp.zeros((w, HEAD_DIM), F32)
    for j in range(c // (2 * w)):
        lo = slice(2 * w * j, 2 * w * j + w)
        hi = slice(2 * w * j + w, 2 * w * (j + 1))
        if forward:
            ref = bc[2 * w * j + w - 1:2 * w * j + w, :]
            qs.append(q[hi] * jnp.exp2(bc[hi] - ref))
            ks += [k[lo] * jnp.exp2(ref - bc[lo]), zeros]
        else:
            ref = bc[2 * w * j + w:2 * w * j + w + 1, :]
            qs.append(q[lo] * jnp.exp2(bc[lo] - ref))
            ks += [zeros, k[hi] * jnp.exp2(ref - bc[hi])]
    qt = jnp.concatenate(qs, axis=0).astype(BF16)
    kt = jnp.concatenate(ks, axis=0).astype(BF16)
    return _dot_nt(qt, kt)


def _base_level(q, k, bc):
    c = q.shape[0]
    blocks = c // BASE_BLOCK
    mid = BASE_BLOCK // 2
    bc3 = bc.reshape(blocks, BASE_BLOCK, HEAD_DIM)
    e = bc3 - bc3[:, mid - 1:mid, :]
    qt = (q.reshape(blocks, BASE_BLOCK, HEAD_DIM) * jnp.exp2(e)).reshape(c, HEAD_DIM).astype(BF16)
    kt = (k.reshape(blocks, BASE_BLOCK, HEAD_DIM) * jnp.exp2(-e)).reshape(c, HEAD_DIM).astype(BF16)
    return _dot_nt(qt, kt)


def _level_products(q, kf, bcf, kb, bcb):
    c = q.shape[0]
    prods = {(0, True): _base_level(q, kf, bcf), (0, False): _base_level(q, kb, bcb)}
    w = BASE_BLOCK
    while w < c:
        prods[(w, True)] = _half_level(q, kf, bcf, w, True)
        prods[(w, False)] = _half_level(q, kb, bcb, w, False)
        w *= 2
    return prods


def _assemble_scores(prods, code_ref):
    c = prods[(0, True)].shape[0]
    nb = c // BASE_BLOCK
    blocks = []
    for r in range(nb):
        rows = slice(r * BASE_BLOCK, (r + 1) * BASE_BLOCK)
        code = code_ref[rows, :]
        f, b = prods[(0, True)][rows], prods[(0, False)][rows]
        blocks.append(jnp.where(code == 0, f + b, jnp.where(code == 1, f, jnp.where(code == -1, b, 0.0))))
    w = BASE_BLOCK
    level = 2
    while w < c:
        for forward, sign in ((True, 1), (False, -1)):
            r_w = prods[(w, forward)]
            for j in range(c // (2 * w)):
                first = 2 * w * j + (w if forward else 0)
                for r8 in range(w // BASE_BLOCK):
                    blk = first // BASE_BLOCK + r8
                    src = slice(j * w + r8 * BASE_BLOCK, j * w + (r8 + 1) * BASE_BLOCK)
                    code = code_ref[blk * BASE_BLOCK:(blk + 1) * BASE_BLOCK, :]
                    blocks[blk] = jnp.where(code == sign * level, r_w[src], blocks[blk])
        w *= 2
        level += 1
    return jnp.concatenate(blocks, axis=0)


def _scan_kernel(n_chunks, v_ref, zf_ref, zb_ref, q_ref, ga_ref, lbf_ref, lbb_ref, hn_ref,
                 s0f_ref, s0b_ref, ya_ref, sf_ref, sb_ref, o_s, qt_s, u_s, d_s, st_s, code_s):
    c = SCAN_CHUNK
    hd = HEAD_DIM
    t_idx = lax.broadcasted_iota(jnp.int32, (c, c), 0)
    s_idx = lax.broadcasted_iota(jnp.int32, (c, c), 1)
    tri_lo = (s_idx <= t_idx).astype(BF16)
    tri_up = (s_idx >= t_idx).astype(BF16)
    x = t_idx ^ s_idx
    lvl = jnp.ones((c, c), jnp.int32)
    w = BASE_BLOCK
    while w < c:
        lvl = lvl + (x >= w).astype(jnp.int32)
        w *= 2
    code_s[...] = jnp.where(t_idx > s_idx, lvl, jnp.where(t_idx < s_idx, -lvl, 0))

    one_m_lbf = 1.0 - lbf_ref[0]
    one_m_lbb = 1.0 - lbb_ref[0]
    hn = hn_ref[0]

    def gate_stage(i):
        rows = pl.ds(pl.multiple_of(i * c, c), c)
        kf, gf = _forget(zf_ref[0, 0, rows, :].astype(F32), one_m_lbf)
        kb, gb = _forget(zb_ref[0, 0, rows, :].astype(F32), one_m_lbb)
        q = _silu(q_ref[0, 0, rows, :].astype(F32))
        return dict(i=i, rows=rows, kf=kf, kb=kb, q=q,
                    bcf=_chunk_cumsum(tri_lo, gf), bcb=_chunk_cumsum(tri_up, gb))

    def decay_stage(s):
        q, kf, kb, bcf, bcb, rows = s["q"], s["kf"], s["kb"], s["bcf"], s["bcb"], s["rows"]
        s["prods"] = _level_products(q, kf, bcf, kb, bcb)
        qt_s[rows, :hd] = (q * jnp.exp2(bcf)).astype(BF16)
        qt_s[rows, hd:] = (q * jnp.exp2(bcb)).astype(BF16)
        ftot = bcf[c - 1:c, :]
        btot = bcb[0:1, :]
        khat = jnp.concatenate([kf * jnp.exp2(ftot - bcf), kb * jnp.exp2(btot - bcb)], axis=1)
        u_s[s["i"]] = _dot_tn(v_ref[0, 0, rows, :], khat.astype(BF16))
        d_s[s["i"]] = jnp.broadcast_to(jnp.concatenate([jnp.exp2(ftot), jnp.exp2(btot)], axis=1),
                                       (BASE_BLOCK, 2 * hd))

    def mix_stage(s):
        p = _assemble_scores(s["prods"], code_s)
        o_s[s["rows"], :] = _dot(p.astype(BF16), v_ref[0, 0, s["rows"], :])

    group = math.gcd(CHUNK_GROUP, n_chunks)

    def chunk_body(g, carry):
        chunks = [gate_stage(g * group + k) for k in range(group)]
        for s in chunks:
            decay_stage(s)
        for s in chunks:
            mix_stage(s)
        return carry

    lax.fori_loop(0, n_chunks // group, chunk_body, 0)

    def fwd_state(i, sf):
        st_s[i, :, :hd] = sf.astype(BF16)
        return sf * d_s[i, 0:1, :hd] + u_s[i, :, :hd]

    def bwd_state(it, sb):
        i = n_chunks - 1 - it
        st_s[i, :, hd:] = sb.astype(BF16)
        return sb * d_s[i, 0:1, hd:] + u_s[i, :, hd:]

    sf_ref[0, 0] = lax.fori_loop(0, n_chunks, fwd_state, s0f_ref[0, 0])
    sb_ref[0, 0] = lax.fori_loop(0, n_chunks, bwd_state, s0b_ref[0, 0])

    def out_body(g, carry):
        inter = []
        for k in range(group):
            i = g * group + k
            rows = pl.ds(pl.multiple_of(i * c, c), c)
            inter.append((rows, _dot_nt(qt_s[rows, :], st_s[i])))
        for rows, o_inter in inter:
            o = o_s[rows, :] + o_inter
            ms = jnp.mean(o * o, axis=-1, keepdims=True)
            y = o * lax.rsqrt(ms + EPS) * hn
            ya_ref[0, rows, :] = (y * _silu(ga_ref[0, 0, rows, :].astype(F32))).astype(BF16)
        return carry

    lax.fori_loop(0, n_chunks // group, out_body, 0)


def _scan(p, lb_f, lb_b, hn_w, s0f_t, s0b_t):
    b, _, length, _ = p.shape
    heads = lb_f.shape[0] // HEAD_DIM
    c = SCAN_CHUNK
    n_chunks = length // c

    def slab(k):
        return pl.BlockSpec((1, 1, length, HEAD_DIM), lambda bi, h: (bi, h + k * heads, 0, 0))

    vec = pl.BlockSpec((1, 1, HEAD_DIM), lambda bi, h: (h, 0, 0))
    state = pl.BlockSpec((1, 1, HEAD_DIM, HEAD_DIM), lambda bi, h: (bi, h, 0, 0))
    return pl.pallas_call(
        functools.partial(_scan_kernel, n_chunks),
        grid=(b, heads),
        in_specs=[slab(0), slab(1), slab(2), slab(3), slab(4), vec, vec, vec, state, state],
        out_specs=[
            pl.BlockSpec((1, length, HEAD_DIM), lambda bi, h: (bi, 0, h)),
            state, state,
        ],
        out_shape=[
            jax.ShapeDtypeStruct((b, length, heads * HEAD_DIM), BF16),
            jax.ShapeDtypeStruct((b, heads, HEAD_DIM, HEAD_DIM), F32),
            jax.ShapeDtypeStruct((b, heads, HEAD_DIM, HEAD_DIM), F32),
        ],
        scratch_shapes=[
            pltpu.VMEM((length, HEAD_DIM), F32),
            pltpu.VMEM((length, 2 * HEAD_DIM), BF16),
            pltpu.VMEM((n_chunks, HEAD_DIM, 2 * HEAD_DIM), F32),
            pltpu.VMEM((n_chunks, BASE_BLOCK, 2 * HEAD_DIM), F32),
            pltpu.VMEM((n_chunks, HEAD_DIM, 2 * HEAD_DIM), BF16),
            pltpu.VMEM((c, c), jnp.int32),
        ],
        compiler_params=pltpu.CompilerParams(
            dimension_semantics=("parallel", "parallel"), vmem_limit_bytes=VMEM_LIMIT),
        name="scan",
    )(p, p, p, p, p,
      lb_f.reshape(heads, 1, HEAD_DIM), lb_b.reshape(heads, 1, HEAD_DIM),
      hn_w.reshape(heads, 1, HEAD_DIM), s0f_t, s0b_t)


def _fourier_kernel(length, u_ref, g_ref, ab_ref, bias_ref, cs_ref, y_ref, tt_s):
    cg = GROUP_DIM
    slabs = cg // LANES

    @pl.when(pl.program_id(1) == 0)
    def _():
        rt = min(512, length)
        for g in range(FOURIER_GROUPS):
            for r in range(length // rt):
                rows = slice(r * rt, (r + 1) * rt)
                ug = jnp.concatenate([u_ref[0, slabs * g + s, rows, :] for s in range(slabs)], axis=1)
                tt_s[rows, g * cg:(g + 1) * cg] = _dot(ug, ab_ref[g, :cg, :]).astype(BF16)
                tt_s[length + r * rt:length + (r + 1) * rt, g * cg:(g + 1) * cg] = (
                    _dot(ug, ab_ref[g, cg:, :]).astype(BF16))

    acc = _dot(cs_ref[...], tt_s[...])
    for j in range(acc.shape[1] // LANES):
        cols = slice(j * LANES, (j + 1) * LANES)
        y = (acc[:, cols] + bias_ref[:, cols]) * _silu(g_ref[0, j].astype(F32))
        y_ref[0, :, cols] = y.astype(BF16)


def _fourier(p, u_block, gate_block, ab, bias, cs):
    b, _, length, _ = p.shape
    width = bias.shape[0]
    slabs = width // LANES
    tm = min(512, length)
    return pl.pallas_call(
        functools.partial(_fourier_kernel, length),
        grid=(b, length // tm),
        in_specs=[
            pl.BlockSpec((1, slabs, length, LANES), lambda bi, i: (bi, u_block, 0, 0)),
            pl.BlockSpec((1, slabs, tm, LANES), lambda bi, i: (bi, gate_block, i, 0)),
            pl.BlockSpec(ab.shape, lambda bi, i: (0, 0, 0)),
            pl.BlockSpec((1, width), lambda bi, i: (0, 0)),
            pl.BlockSpec((tm, 2 * length), lambda bi, i: (i, 0)),
        ],
        out_specs=pl.BlockSpec((1, tm, width), lambda bi, i: (bi, i, 0)),
        out_shape=jax.ShapeDtypeStruct((b, length, width), BF16),
        scratch_shapes=[pltpu.VMEM((2 * length, width), BF16)],
        compiler_params=pltpu.CompilerParams(
            dimension_semantics=("parallel", "arbitrary"), vmem_limit_bytes=VMEM_LIMIT),
        name="fourier",
    )(p, p, ab, bias.reshape(1, width), cs)


def _seq_dft_matrix(length):
    cos, sin = _dft_tables(length)
    norm = 1.0 / math.sqrt(length * GROUP_DIM)
    return jnp.concatenate([cos * norm, sin * (-norm)], axis=1).astype(BF16)


def _outproj_kernel(add_pos, final_norm, *refs):
    refs = list(refs)
    ya_ref, yf_ref, w_ref, x_ref, gate_ref = refs[:5]
    rest = refs[5:]
    pos_ref = rest.pop(0) if add_pos else None
    fnw_ref = rest.pop(0) if final_norm else None
    o_ref = rest.pop(0)
    wa = ya_ref.shape[2]
    out = _dot(ya_ref[0], w_ref[:wa, :]) + _dot(yf_ref[0], w_ref[wa:, :])
    x = x_ref[0]
    if add_pos:
        x = x + pos_ref[...]
    x = x + gate_ref[0] * out
    if final_norm:
        ms = jnp.mean(x * x, axis=-1, keepdims=True)
        x = x * lax.rsqrt(ms + EPS) * fnw_ref[...]
    o_ref[0] = x


def _outproj(y_a, y_f, w_bf16, x, gate, pos, final_norm_w):
    b, length, d = x.shape
    wa, wf = y_a.shape[2], y_f.shape[2]
    tm = min(512, length)
    add_pos = pos is not None
    final_norm = final_norm_w is not None
    in_specs = [
        pl.BlockSpec((1, tm, wa), lambda bi, i: (bi, i, 0)),
        pl.BlockSpec((1, tm, wf), lambda bi, i: (bi, i, 0)),
        pl.BlockSpec((wa + wf, d), lambda bi, i: (0, 0)),
        pl.BlockSpec((1, tm, d), lambda bi, i: (bi, i, 0)),
        pl.BlockSpec((1, 1, d), lambda bi, i: (bi, 0, 0)),
    ]
    args = [y_a, y_f, w_bf16, x, gate.reshape(b, 1, d)]
    if add_pos:
        in_specs.append(pl.BlockSpec((tm, d), lambda bi, i: (i, 0)))
        args.append(pos)
    if final_norm:
        in_specs.append(pl.BlockSpec((1, d), lambda bi, i: (0, 0)))
        args.append(final_norm_w.reshape(1, d))
    return pl.pallas_call(
        functools.partial(_outproj_kernel, add_pos, final_norm),
        grid=(b, length // tm),
        in_specs=in_specs,
        out_specs=pl.BlockSpec((1, tm, d), lambda bi, i: (bi, i, 0)),
        out_shape=jax.ShapeDtypeStruct((b, length, d), F32),
        compiler_params=pltpu.CompilerParams(
            dimension_semantics=("parallel", "parallel"), vmem_limit_bytes=VMEM_LIMIT),
        name="outproj",
    )(*args)


def kernel(x, c, ctx, c_ctx, norm_w, w_ada, b_ada, w_in, lower_bounds, hgrn_norm_w,
           w_fourier, b_fourier, w_out, final_norm_w):
    b, length, d = x.shape
    depth = w_in.shape[0]
    hgrn_width = hgrn_norm_w.shape[1]
    heads = hgrn_width // HEAD_DIM
    fourier_width = b_fourier.shape[1]
    u_block = 5 * hgrn_width // fourier_width
    gate_block = u_block + 1
    ctx_len = ctx.shape[1]

    pos = _sincos_2d(length, d)
    lbs = _lower_bounds(lower_bounds)
    rows = ((b + 1 + 7) // 8) * 8
    c_rows = jnp.zeros((rows, d), F32).at[:b].set(c).at[b].set(c_ctx)
    mod = _ada_all(c_rows, w_ada, b_ada)
    ab = _fold_fourier_weights(w_fourier)
    cs_x = _seq_dft_matrix(length)
    cs_c = _seq_dft_matrix(ctx_len)
    w_in_b = w_in.astype(BF16)
    w_out_b = w_out.astype(BF16)
    zero_state = jnp.zeros((b, heads, HEAD_DIM, HEAD_DIM), F32)

    for l in range(depth):
        shift, scale, gate = (mod[l, :b, k * d:(k + 1) * d] for k in range(3))
        shift_c, scale_c, gate_c = (jnp.broadcast_to(mod[l, b:b + 1, k * d:(k + 1) * d], (b, d))
                                    for k in range(3))
        last = l == depth - 1
        pc = _inproj(ctx, None, norm_w[l], scale_c, shift_c, w_in_b[l])
        ya_c, sf_t, sb_t = _scan(pc, lbs[0, l], lbs[1, l], hgrn_norm_w[l], zero_state, zero_state)
        px = _inproj(x, pos if l == 0 else None, norm_w[l], scale, shift, w_in_b[l])
        ya_x, _, _ = _scan(px, lbs[0, l], lbs[1, l], hgrn_norm_w[l], sf_t, sb_t)
        yf_x = _fourier(px, u_block, gate_block, ab[l], b_fourier[l], cs_x)
        x = _outproj(ya_x, yf_x, w_out_b[l], x, gate, pos if l == 0 else None,
                     final_norm_w if last else None)
        if not last:
            yf_c = _fourier(pc, u_block, gate_block, ab[l], b_fourier[l], cs_c)
            ctx = _outproj(ya_c, yf_c, w_out_b[l], ctx, gate_c, None, None)
    return x
```

```python
import functools
import math

import jax
import jax.numpy as jnp
from jax import lax
from jax.experimental import pallas as pl
from jax.experimental.pallas import tpu as pltpu

F32 = jnp.float32
BF16 = jnp.bfloat16

LANES = 128
HEAD_DIM = 128
GROUP_DIM = 256
FOURIER_GROUPS = 4
GRID_W = 64
POS_BASE = 10000.0
EPS = 1e-6
F_MIN = 1e-6
LOG2_E = 1.4426950408889634
SCAN_CHUNK = 128
BASE_BLOCK = 8
CHUNK_GROUP = 8
VMEM_LIMIT = 56 * 1024 * 1024


def _sigmoid(z):
    return 1.0 / (1.0 + jnp.exp(-z))


def _silu(z):
    return z * _sigmoid(z)


def _dot(a, b):
    return jnp.dot(a, b, preferred_element_type=F32)


def _dot_nt(a, b):
    return lax.dot_general(a, b, (((1,), (1,)), ((), ())), preferred_element_type=F32)


def _dot_tn(a, b):
    return lax.dot_general(a, b, (((0,), (0,)), ((), ())), preferred_element_type=F32)


def _ada_kernel(c_ref, w_ref, b_ref, o_ref):
    sc = _silu(c_ref[...]).astype(BF16)
    o_ref[0] = _dot(sc, w_ref[0].astype(BF16)) + b_ref[0]


def _ada_all(c_rows, w_ada, b_ada):
    depth, d, n = w_ada.shape
    rows = c_rows.shape[0]
    tn = 1024
    return pl.pallas_call(
        _ada_kernel,
        grid=(depth, n // tn),
        in_specs=[
            pl.BlockSpec((rows, d), lambda l, j: (0, 0)),
            pl.BlockSpec((1, d, tn), lambda l, j: (l, 0, j)),
            pl.BlockSpec((1, 1, tn), lambda l, j: (l, 0, j)),
        ],
        out_specs=pl.BlockSpec((1, rows, tn), lambda l, j: (l, 0, j)),
        out_shape=jax.ShapeDtypeStruct((depth, rows, n), F32),
        name="ada_mod",
    )(c_rows, w_ada, b_ada.reshape(depth, 1, n))


def _lower_bound_kernel(depth, lb_ref, o_ref):
    for dr in range(2):
        rows = [lb_ref[dr * depth + l:dr * depth + l + 1, :] for l in range(depth)]
        mx = functools.reduce(jnp.maximum, rows)
        ex = [jnp.exp(r - mx) for r in rows]
        den = functools.reduce(lambda a, b: a + b, ex)
        p = [e / den for e in ex]
        run = p[0]
        for l in range(depth):
            if l > 0:
                run = run + p[l]
            o_ref[dr * depth + l:dr * depth + l + 1, :] = run - p[0]


def _lower_bounds(lower_bounds):
    two, depth, width = lower_bounds.shape
    out = pl.pallas_call(
        functools.partial(_lower_bound_kernel, depth),
        out_shape=jax.ShapeDtypeStruct((two * depth, width), F32),
        name="lower_bounds",
    )(lower_bounds.reshape(two * depth, width))
    return out.reshape(two, depth, width)


def _fold_kernel(cs_ref, w_ref, o_ref):
    o_ref[0, 0] = jnp.dot(cs_ref[...], w_ref[0, 0], preferred_element_type=F32,
                          precision=lax.Precision.HIGHEST).astype(BF16)


def _fold_fourier_weights(w_fourier):
    depth, groups, cg, _ = w_fourier.shape
    cs = jnp.concatenate(_dft_tables(cg), axis=0)
    return pl.pallas_call(
        _fold_kernel,
        grid=(depth, groups),
        in_specs=[
            pl.BlockSpec((2 * cg, cg), lambda l, g: (0, 0)),
            pl.BlockSpec((1, 1, cg, cg), lambda l, g: (l, g, 0, 0)),
        ],
        out_specs=pl.BlockSpec((1, 1, 2 * cg, cg), lambda l, g: (l, g, 0, 0)),
        out_shape=jax.ShapeDtypeStruct((depth, groups, 2 * cg, cg), BF16),
        name="fold_fourier",
    )(cs, w_fourier)


def _dft_tables(n):
    j = lax.broadcasted_iota(jnp.int32, (n, n), 0)
    k = lax.broadcasted_iota(jnp.int32, (n, n), 1)
    ang = ((j * k) % n).astype(F32) * (2.0 * math.pi / n)
    return jnp.cos(ang), jnp.sin(ang)


def _sincos_2d(length, dim):
    rows = length // GRID_W
    row = jnp.repeat(jnp.arange(rows, dtype=F32), GRID_W)
    col = jnp.tile(jnp.arange(GRID_W, dtype=F32), rows)
    quarter = dim // 4
    omega = jnp.power(POS_BASE, -jnp.arange(quarter, dtype=F32) / quarter)

    def axis_code(p):
        ang = p[:, None] * omega[None, :]
        return jnp.concatenate([jnp.sin(ang), jnp.cos(ang)], axis=-1)

    return jnp.concatenate([axis_code(row), axis_code(col)], axis=-1)


G_V, G_Q, G_KF, G_KB, G_GF, G_GB, G_GA, G_U, G_GATE_F = 0, 1, 2, 3, 4, 6, 8, 9, 10
N_SLAB_GROUPS = 11
INPROJ_SUBTILE = 256


def _inproj_kernel(add_pos, width, *refs):
    if add_pos:
        x_ref, pos_ref, nw_ref, sc_ref, sh_ref, lbf_ref, lbb_ref, w_ref, o_ref = refs
    else:
        x_ref, nw_ref, sc_ref, sh_ref, lbf_ref, lbb_ref, w_ref, o_ref = refs
    x = x_ref[0]
    if add_pos:
        x = x + pos_ref[...]
    ms = jnp.mean(x * x, axis=-1, keepdims=True)
    y = x * lax.rsqrt(ms + EPS) * nw_ref[...]
    h = (y * (1.0 + sc_ref[0]) + sh_ref[0]).astype(BF16)
    per = width // LANES
    sub = INPROJ_SUBTILE // LANES

    def project(j, t):
        lo = j * width + t * INPROJ_SUBTILE
        return _dot(h, w_ref[:, lo:lo + INPROJ_SUBTILE])

    def put(group, t, values):
        for s in range(sub):
            o_ref[0, group * per + t * sub + s] = values[:, s * LANES:(s + 1) * LANES].astype(BF16)

    def put_forget(k_group, g_group, t, z, lb_ref):
        one_minus_lb = 1.0 - lb_ref[:, t * INPROJ_SUBTILE:(t + 1) * INPROJ_SUBTILE]
        k = one_minus_lb * (1.0 / (1.0 + jnp.exp(z)))
        g = jnp.log(jnp.maximum(1.0 - k, F_MIN)) * LOG2_E
        hi = g.astype(BF16)
        lo = (g - hi.astype(F32)).astype(BF16)
        put(k_group, t, k)
        for s in range(sub):
            o_ref[0, g_group * per + 2 * (t * sub + s)] = hi[:, s * LANES:(s + 1) * LANES]
            o_ref[0, g_group * per + 2 * (t * sub + s) + 1] = lo[:, s * LANES:(s + 1) * LANES]

    for t in range(width // INPROJ_SUBTILE):
        put(G_V, t, project(0, t))
        put_forget(G_KF, G_GF, t, project(1, t), lbf_ref)
        put_forget(G_KB, G_GB, t, project(2, t), lbb_ref)
        put(G_Q, t, _silu(project(3, t)))
        put(G_GA, t, _silu(project(4, t)))
        put(G_U, t, project(5, t))
        put(G_GATE_F, t, _silu(project(6, t)))


def _inproj(x, pos, norm_w, scale, shift, lb_f, lb_b, w_bf16):
    b, length, d = x.shape
    n = w_bf16.shape[1]
    width = lb_f.shape[0]
    assert n == 7 * width
    tm = min(256, length)
    add_pos = pos is not None
    in_specs = [pl.BlockSpec((1, tm, d), lambda bi, i: (bi, i, 0))]
    args = [x]
    if add_pos:
        in_specs.append(pl.BlockSpec((tm, d), lambda bi, i: (i, 0)))
        args.append(pos)
    in_specs += [
        pl.BlockSpec((1, d), lambda bi, i: (0, 0)),
        pl.BlockSpec((1, 1, d), lambda bi, i: (bi, 0, 0)),
        pl.BlockSpec((1, 1, d), lambda bi, i: (bi, 0, 0)),
        pl.BlockSpec((1, width), lambda bi, i: (0, 0)),
        pl.BlockSpec((1, width), lambda bi, i: (0, 0)),
        pl.BlockSpec((d, n), lambda bi, i: (0, 0), pipeline_mode=pl.Buffered(1)),
    ]
    args += [norm_w.reshape(1, d), scale.reshape(b, 1, d), shift.reshape(b, 1, d),
             lb_f.reshape(1, width), lb_b.reshape(1, width), w_bf16]
    n_slabs = N_SLAB_GROUPS * width // LANES
    return pl.pallas_call(
        functools.partial(_inproj_kernel, add_pos, width),
        grid=(b, length // tm),
        in_specs=in_specs,
        out_specs=pl.BlockSpec((1, n_slabs, tm, LANES), lambda bi, i: (bi, 0, i, 0)),
        out_shape=jax.ShapeDtypeStruct((b, n_slabs, length, LANES), BF16),
        compiler_params=pltpu.CompilerParams(
            dimension_semantics=("parallel", "parallel"), vmem_limit_bytes=VMEM_LIMIT),
        name="inproj",
    )(*args)


def _chunk_cumsum(tri, g_ref, rows):
    out = _dot(tri, jnp.concatenate([g_ref[0, 0, rows, :], g_ref[0, 1, rows, :]], axis=1))
    return out[:, :HEAD_DIM] + out[:, HEAD_DIM:]


def _half_level(q, k, bc, w, forward):
    c = q.shape[0]
    qs, ks = [], []
    zeros = jnp.zeros((w, HEAD_DIM), F32)
    for j in range(c // (2 * w)):
        lo = slice(2 * w * j, 2 * w * j + w)
        hi = slice(2 * w * j + w, 2 * w * (j + 1))
        if forward:
            ref = bc[2 * w * j + w - 1:2 * w * j + w, :]
            qs.append(q[hi] * jnp.exp2(bc[hi] - ref))
            ks += [k[lo] * jnp.exp2(ref - bc[lo]), zeros]
        else:
            ref = bc[2 * w * j + w:2 * w * j + w + 1, :]
            qs.append(q[lo] * jnp.exp2(bc[lo] - ref))
            ks += [zeros, k[hi] * jnp.exp2(ref - bc[hi])]
    qt = jnp.concatenate(qs, axis=0).astype(BF16)
    kt = jnp.concatenate(ks, axis=0).astype(BF16)
    return _dot_nt(qt, kt)


def _base_level(q, k, bc):
    c = q.shape[0]
    blocks = c // BASE_BLOCK
    mid = BASE_BLOCK // 2
    bc3 = bc.reshape(blocks, BASE_BLOCK, HEAD_DIM)
    e = bc3 - bc3[:, mid - 1:mid, :]
    qt = (q.reshape(blocks, BASE_BLOCK, HEAD_DIM) * jnp.exp2(e)).reshape(c, HEAD_DIM).astype(BF16)
    kt = (k.reshape(blocks, BASE_BLOCK, HEAD_DIM) * jnp.exp2(-e)).reshape(c, HEAD_DIM).astype(BF16)
    return _dot_nt(qt, kt)


def _level_products(q, kf, bcf, kb, bcb):
    c = q.shape[0]
    prods = {(0, True): _base_level(q, kf, bcf), (0, False): _base_level(q, kb, bcb)}
    w = BASE_BLOCK
    while w < c:
        prods[(w, True)] = _half_level(q, kf, bcf, w, True)
        prods[(w, False)] = _half_level(q, kb, bcb, w, False)
        w *= 2
    return prods


def _assemble_scores(prods, code_ref):
    c = prods[(0, True)].shape[0]
    nb = c // BASE_BLOCK
    blocks = []
    for r in range(nb):
        rows = slice(r * BASE_BLOCK, (r + 1) * BASE_BLOCK)
        code = code_ref[rows, :]
        f, b = prods[(0, True)][rows], prods[(0, False)][rows]
        blocks.append(jnp.where(code == 0, f + b, jnp.where(code == 1, f, jnp.where(code == -1, b, 0.0))))
    w = BASE_BLOCK
    level = 2
    while w < c:
        for forward, sign in ((True, 1), (False, -1)):
            r_w = prods[(w, forward)]
            for j in range(c // (2 * w)):
                first = 2 * w * j + (w if forward else 0)
                for r8 in range(w // BASE_BLOCK):
                    blk = first // BASE_BLOCK + r8
                    src = slice(j * w + r8 * BASE_BLOCK, j * w + (r8 + 1) * BASE_BLOCK)
                    code = code_ref[blk * BASE_BLOCK:(blk + 1) * BASE_BLOCK, :]
                    blocks[blk] = jnp.where(code == sign * level, r_w[src], blocks[blk])
        w *= 2
        level += 1
    return jnp.concatenate(blocks, axis=0)


def _scan_kernel(n_chunks, v_ref, q_ref, kf_ref, kb_ref, gf_ref, gb_ref, ga_ref, hn_ref,
                 s0f_ref, s0b_ref, ya_ref, sf_ref, sb_ref, o_s, qt_s, u_s, d_s, st_s, code_s):
    c = SCAN_CHUNK
    hd = HEAD_DIM
    t_idx = lax.broadcasted_iota(jnp.int32, (c, c), 0)
    s_idx = lax.broadcasted_iota(jnp.int32, (c, c), 1)
    tri_lo = (s_idx <= t_idx).astype(BF16)
    tri_up = (s_idx >= t_idx).astype(BF16)
    x = t_idx ^ s_idx
    lvl = jnp.ones((c, c), jnp.int32)
    w = BASE_BLOCK
    while w < c:
        lvl = lvl + (x >= w).astype(jnp.int32)
        w *= 2
    code_s[...] = jnp.where(t_idx > s_idx, lvl, jnp.where(t_idx < s_idx, -lvl, 0))

    hn = hn_ref[0]

    def gate_stage(i):
        rows = pl.ds(pl.multiple_of(i * c, c), c)
        return dict(i=i, rows=rows,
                    kf=kf_ref[0, 0, rows, :].astype(F32), kb=kb_ref[0, 0, rows, :].astype(F32),
                    q=q_ref[0, 0, rows, :].astype(F32),
                    bcf=_chunk_cumsum(tri_lo, gf_ref, rows), bcb=_chunk_cumsum(tri_up, gb_ref, rows))

    def decay_stage(s):
        q, kf, kb, bcf, bcb, rows = s["q"], s["kf"], s["kb"], s["bcf"], s["bcb"], s["rows"]
        s["prods"] = _level_products(q, kf, bcf, kb, bcb)
        qt_s[rows, :hd] = (q * jnp.exp2(bcf)).astype(BF16)
        qt_s[rows, hd:] = (q * jnp.exp2(bcb)).astype(BF16)
        ftot = bcf[c - 1:c, :]
        btot = bcb[0:1, :]
        khat = jnp.concatenate([kf * jnp.exp2(ftot - bcf), kb * jnp.exp2(btot - bcb)], axis=1)
        u_s[s["i"]] = _dot_tn(v_ref[0, 0, rows, :], khat.astype(BF16))
        d_s[s["i"]] = jnp.broadcast_to(jnp.concatenate([jnp.exp2(ftot), jnp.exp2(btot)], axis=1),
                                       (BASE_BLOCK, 2 * hd))

    def mix_stage(s):
        p = _assemble_scores(s["prods"], code_s)
        o_s[s["rows"], :] = _dot(p.astype(BF16), v_ref[0, 0, s["rows"], :])

    group = math.gcd(CHUNK_GROUP, n_chunks)

    def chunk_body(g, carry):
        chunks = [gate_stage(g * group + k) for k in range(group)]
        for s in chunks:
            decay_stage(s)
        for s in chunks:
            mix_stage(s)
        return carry

    lax.fori_loop(0, n_chunks // group, chunk_body, 0)

    def fwd_state(i, sf):
        st_s[i, :, :hd] = sf.astype(BF16)
        return sf * d_s[i, 0:1, :hd] + u_s[i, :, :hd]

    def bwd_state(it, sb):
        i = n_chunks - 1 - it
        st_s[i, :, hd:] = sb.astype(BF16)
        return sb * d_s[i, 0:1, hd:] + u_s[i, :, hd:]

    sf_ref[0, 0] = lax.fori_loop(0, n_chunks, fwd_state, s0f_ref[0, 0])
    sb_ref[0, 0] = lax.fori_loop(0, n_chunks, bwd_state, s0b_ref[0, 0])

    def out_body(g, carry):
        inter = []
        for k in range(group):
            i = g * group + k
            rows = pl.ds(pl.multiple_of(i * c, c), c)
            inter.append((rows, _dot_nt(qt_s[rows, :], st_s[i])))
        for rows, o_inter in inter:
            o = o_s[rows, :] + o_inter
            ms = jnp.mean(o * o, axis=-1, keepdims=True)
            y = o * lax.rsqrt(ms + EPS) * hn
            ya_ref[0, rows, :] = (y * ga_ref[0, 0, rows, :].astype(F32)).astype(BF16)
        return carry

    lax.fori_loop(0, n_chunks // group, out_body, 0)


def _scan(p, hn_w, s0f_t, s0b_t):
    b, _, length, _ = p.shape
    heads = hn_w.shape[0] // HEAD_DIM
    c = SCAN_CHUNK
    n_chunks = length // c

    def slab(group):
        return pl.BlockSpec((1, 1, length, HEAD_DIM), lambda bi, h: (bi, h + group * heads, 0, 0))

    def slab_pair(group):
        return pl.BlockSpec((1, 2, length, HEAD_DIM), lambda bi, h: (bi, h + group * heads // 2, 0, 0))

    vec = pl.BlockSpec((1, 1, HEAD_DIM), lambda bi, h: (h, 0, 0))
    state = pl.BlockSpec((1, 1, HEAD_DIM, HEAD_DIM), lambda bi, h: (bi, h, 0, 0))
    return pl.pallas_call(
        functools.partial(_scan_kernel, n_chunks),
        grid=(b, heads),
        in_specs=[slab(G_V), slab(G_Q), slab(G_KF), slab(G_KB), slab_pair(G_GF), slab_pair(G_GB),
                  slab(G_GA), vec, state, state],
        out_specs=[
            pl.BlockSpec((1, length, HEAD_DIM), lambda bi, h: (bi, 0, h)),
            state, state,
        ],
        out_shape=[
            jax.ShapeDtypeStruct((b, length, heads * HEAD_DIM), BF16),
            jax.ShapeDtypeStruct((b, heads, HEAD_DIM, HEAD_DIM), F32),
            jax.ShapeDtypeStruct((b, heads, HEAD_DIM, HEAD_DIM), F32),
        ],
        scratch_shapes=[
            pltpu.VMEM((length, HEAD_DIM), F32),
            pltpu.VMEM((length, 2 * HEAD_DIM), BF16),
            pltpu.VMEM((n_chunks, HEAD_DIM, 2 * HEAD_DIM), F32),
            pltpu.VMEM((n_chunks, BASE_BLOCK, 2 * HEAD_DIM), F32),
            pltpu.VMEM((n_chunks, HEAD_DIM, 2 * HEAD_DIM), BF16),
            pltpu.VMEM((c, c), jnp.int32),
        ],
        compiler_params=pltpu.CompilerParams(
            dimension_semantics=("parallel", "parallel"), vmem_limit_bytes=VMEM_LIMIT),
        name="scan",
    )(p, p, p, p, p, p, p, hn_w.reshape(heads, 1, HEAD_DIM), s0f_t, s0b_t)


def _fourier_kernel(length, u_ref, g_ref, ab_ref, bias_ref, cs_ref, y_ref, tt_s):
    cg = GROUP_DIM
    slabs = cg // LANES
    half = length // 2

    @pl.when(pl.program_id(1) == 0)
    def _():
        l_idx = lax.broadcasted_iota(jnp.int32, (half, half), 0)
        m_idx = lax.broadcasted_iota(jnp.int32, (half, half), 1)
        rev = ((l_idx + m_idx == half) & (l_idx >= 1)).astype(BF16)
        for g in range(FOURIER_GROUPS):
            cols = slice(g * cg, (g + 1) * cg)
            lower = jnp.concatenate([u_ref[0, slabs * g + s, :half, :] for s in range(slabs)], axis=1)
            upper = jnp.concatenate([u_ref[0, slabs * g + s, half:, :] for s in range(slabs)], axis=1)
            mirrored = _dot(rev, upper)
            lower = lower.astype(F32)
            tt_s[:half, cols] = _dot((lower + mirrored).astype(BF16), ab_ref[g, :cg, :]).astype(BF16)
            tt_s[half:, cols] = _dot((lower - mirrored).astype(BF16), ab_ref[g, cg:, :]).astype(BF16)
            tt_s[half:half + 1, cols] = _dot(upper[:16], ab_ref[g, :cg, :])[0:1].astype(BF16)

    acc = _dot(cs_ref[...], tt_s[...])
    for j in range(acc.shape[1] // LANES):
        cols = slice(j * LANES, (j + 1) * LANES)
        y = (acc[:, cols] + bias_ref[:, cols]) * g_ref[0, j].astype(F32)
        y_ref[0, :, cols] = y.astype(BF16)


def _fourier(p, u_block, gate_block, ab, bias, cs):
    b, _, length, _ = p.shape
    width = bias.shape[0]
    slabs = width // LANES
    tm = min(512, length)
    return pl.pallas_call(
        functools.partial(_fourier_kernel, length),
        grid=(b, length // tm),
        in_specs=[
            pl.BlockSpec((1, slabs, length, LANES), lambda bi, i: (bi, u_block, 0, 0)),
            pl.BlockSpec((1, slabs, tm, LANES), lambda bi, i: (bi, gate_block, i, 0)),
            pl.BlockSpec(ab.shape, lambda bi, i: (0, 0, 0)),
            pl.BlockSpec((1, width), lambda bi, i: (0, 0)),
            pl.BlockSpec((tm, length), lambda bi, i: (i, 0)),
        ],
        out_specs=pl.BlockSpec((1, tm, width), lambda bi, i: (bi, i, 0)),
        out_shape=jax.ShapeDtypeStruct((b, length, width), BF16),
        scratch_shapes=[pltpu.VMEM((length, width), BF16)],
        compiler_params=pltpu.CompilerParams(
            dimension_semantics=("parallel", "arbitrary"), vmem_limit_bytes=VMEM_LIMIT),
        name="fourier",
    )(p, p, ab, bias.reshape(1, width), cs)


def _seq_dft_matrix(length):
    half = length // 2
    row = lax.broadcasted_iota(jnp.int32, (length, length), 0)
    col = lax.broadcasted_iota(jnp.int32, (length, length), 1)
    freq = jnp.where(col <= half, col, col - half)
    ang = ((row * freq) % length).astype(F32) * (2.0 * math.pi / length)
    norm = 1.0 / math.sqrt(length * GROUP_DIM)
    return jnp.where(col <= half, jnp.cos(ang) * norm, jnp.sin(ang) * (-norm)).astype(BF16)


def _outproj_kernel(add_pos, final_norm, *refs):
    refs = list(refs)
    ya_ref, yf_ref, w_ref, x_ref, gate_ref = refs[:5]
    rest = refs[5:]
    pos_ref = rest.pop(0) if add_pos else None
    fnw_ref = rest.pop(0) if final_norm else None
    o_ref = rest.pop(0)
    wa = ya_ref.shape[2]
    out = _dot(ya_ref[0], w_ref[:wa, :]) + _dot(yf_ref[0], w_ref[wa:, :])
    x = x_ref[0]
    if add_pos:
        x = x + pos_ref[...]
    x = x + gate_ref[0] * out
    if final_norm:
        ms = jnp.mean(x * x, axis=-1, keepdims=True)
        x = x * lax.rsqrt(ms + EPS) * fnw_ref[...]
    o_ref[0] = x


def _outproj(y_a, y_f, w_bf16, x, gate, pos, final_norm_w):
    b, length, d = x.shape
    wa, wf = y_a.shape[2], y_f.shape[2]
    tm = min(512, length)
    add_pos = pos is not None
    final_norm = final_norm_w is not None
    in_specs = [
        pl.BlockSpec((1, tm, wa), lambda bi, i: (bi, i, 0)),
        pl.BlockSpec((1, tm, wf), lambda bi, i: (bi, i, 0)),
        pl.BlockSpec((wa + wf, d), lambda bi, i: (0, 0)),
        pl.BlockSpec((1, tm, d), lambda bi, i: (bi, i, 0)),
        pl.BlockSpec((1, 1, d), lambda bi, i: (bi, 0, 0)),
    ]
    args = [y_a, y_f, w_bf16, x, gate.reshape(b, 1, d)]
    if add_pos:
        in_specs.append(pl.BlockSpec((tm, d), lambda bi, i: (i, 0)))
        args.append(pos)
    if final_norm:
        in_specs.append(pl.BlockSpec((1, d), lambda bi, i: (0, 0)))
        args.append(final_norm_w.reshape(1, d))
    return pl.pallas_call(
        functools.partial(_outproj_kernel, add_pos, final_norm),
        grid=(b, length // tm),
        in_specs=in_specs,
        out_specs=pl.BlockSpec((1, tm, d), lambda bi, i: (bi, i, 0)),
        out_shape=jax.ShapeDtypeStruct((b, length, d), F32),
        compiler_params=pltpu.CompilerParams(
            dimension_semantics=("parallel", "parallel"), vmem_limit_bytes=VMEM_LIMIT),
        name="outproj",
    )(*args)


def kernel(x, c, ctx, c_ctx, norm_w, w_ada, b_ada, w_in, lower_bounds, hgrn_norm_w,
           w_fourier, b_fourier, w_out, final_norm_w):
    b, length, d = x.shape
    depth = w_in.shape[0]
    hgrn_width = hgrn_norm_w.shape[1]
    heads = hgrn_width // HEAD_DIM
    fourier_width = b_fourier.shape[1]
    u_block = G_U * hgrn_width // fourier_width
    gate_block = G_GATE_F * hgrn_width // fourier_width
    ctx_len = ctx.shape[1]

    pos = _sincos_2d(length, d)
    lbs = _lower_bounds(lower_bounds)
    rows = ((b + 1 + 7) // 8) * 8
    c_rows = jnp.zeros((rows, d), F32).at[:b].set(c).at[b].set(c_ctx)
    mod = _ada_all(c_rows, w_ada, b_ada)
    ab = _fold_fourier_weights(w_fourier)
    cs_x = _seq_dft_matrix(length)
    cs_c = _seq_dft_matrix(ctx_len)
    w_in_b = w_in.astype(BF16)
    w_out_b = w_out.astype(BF16)
    zero_state = jnp.zeros((b, heads, HEAD_DIM, HEAD_DIM), F32)

    for l in range(depth):
        shift, scale, gate = (mod[l, :b, k * d:(k + 1) * d] for k in range(3))
        shift_c, scale_c, gate_c = (jnp.broadcast_to(mod[l, b:b + 1, k * d:(k + 1) * d], (b, d))
                                    for k in range(3))
        last = l == depth - 1
        pc = _inproj(ctx, None, norm_w[l], scale_c, shift_c, lbs[0, l], lbs[1, l], w_in_b[l])
        ya_c, sf_t, sb_t = _scan(pc, hgrn_norm_w[l], zero_state, zero_state)
        px = _inproj(x, pos if l == 0 else None, norm_w[l], scale, shift, lbs[0, l], lbs[1, l], w_in_b[l])
        ya_x, _, _ = _scan(px, hgrn_norm_w[l], sf_t, sb_t)
        yf_x = _fourier(px, u_block, gate_block, ab[l], b_fourier[l], cs_x)
        x = _outproj(ya_x, yf_x, w_out_b[l], x, gate, pos if l == 0 else None,
                     final_norm_w if last else None)
        if not last:
            yf_c = _fourier(pc, u_block, gate_block, ab[l], b_fourier[l], cs_c)
            ctx = _outproj(ya_c, yf_c, w_out_b[l], ctx, gate_c, None, None)
    return x
```

```python
import functools
import math

import jax
import jax.numpy as jnp
from jax import lax
from jax.experimental import pallas as pl
from jax.experimental.pallas import tpu as pltpu

F32 = jnp.float32
BF16 = jnp.bfloat16

LANES = 128
HEAD_DIM = 128
GROUP_DIM = 256
FOURIER_GROUPS = 4
GRID_W = 64
POS_BASE = 10000.0
EPS = 1e-6
F_MIN = 1e-6
LOG2_E = 1.4426950408889634
SCAN_CHUNK = 128
BASE_BLOCK = 8
CHUNK_GROUP = 16
VMEM_LIMIT = 56 * 1024 * 1024


def _sigmoid(z):
    return 1.0 / (1.0 + jnp.exp(-z))


def _silu(z):
    return z * _sigmoid(z)


def _dot(a, b):
    return jnp.dot(a, b, preferred_element_type=F32)


def _dot_nt(a, b):
    return lax.dot_general(a, b, (((1,), (1,)), ((), ())), preferred_element_type=F32)


def _dot_tn(a, b):
    return lax.dot_general(a, b, (((0,), (0,)), ((), ())), preferred_element_type=F32)


def _ada_kernel(c_ref, w_ref, b_ref, o_ref):
    sc = _silu(c_ref[...]).astype(BF16)
    o_ref[0] = _dot(sc, w_ref[0].astype(BF16)) + b_ref[0]


def _ada_all(c_rows, w_ada, b_ada):
    depth, d, n = w_ada.shape
    rows = c_rows.shape[0]
    tn = 1024
    return pl.pallas_call(
        _ada_kernel,
        grid=(depth, n // tn),
        in_specs=[
            pl.BlockSpec((rows, d), lambda l, j: (0, 0)),
            pl.BlockSpec((1, d, tn), lambda l, j: (l, 0, j)),
            pl.BlockSpec((1, 1, tn), lambda l, j: (l, 0, j)),
        ],
        out_specs=pl.BlockSpec((1, rows, tn), lambda l, j: (l, 0, j)),
        out_shape=jax.ShapeDtypeStruct((depth, rows, n), F32),
        name="ada_mod",
    )(c_rows, w_ada, b_ada.reshape(depth, 1, n))


def _lower_bound_kernel(depth, lb_ref, o_ref):
    for dr in range(2):
        rows = [lb_ref[dr * depth + l:dr * depth + l + 1, :] for l in range(depth)]
        mx = functools.reduce(jnp.maximum, rows)
        ex = [jnp.exp(r - mx) for r in rows]
        den = functools.reduce(lambda a, b: a + b, ex)
        p = [e / den for e in ex]
        run = p[0]
        for l in range(depth):
            if l > 0:
                run = run + p[l]
            o_ref[dr * depth + l:dr * depth + l + 1, :] = run - p[0]


def _lower_bounds(lower_bounds):
    two, depth, width = lower_bounds.shape
    out = pl.pallas_call(
        functools.partial(_lower_bound_kernel, depth),
        out_shape=jax.ShapeDtypeStruct((two * depth, width), F32),
        name="lower_bounds",
    )(lower_bounds.reshape(two * depth, width))
    return out.reshape(two, depth, width)


def _fold_kernel(cs_ref, w_ref, o_ref):
    o_ref[0, 0] = jnp.dot(cs_ref[...], w_ref[0, 0], preferred_element_type=F32,
                          precision=lax.Precision.HIGHEST).astype(BF16)


def _fold_fourier_weights(w_fourier):
    depth, groups, cg, _ = w_fourier.shape
    cs = jnp.concatenate(_dft_tables(cg), axis=0)
    return pl.pallas_call(
        _fold_kernel,
        grid=(depth, groups),
        in_specs=[
            pl.BlockSpec((2 * cg, cg), lambda l, g: (0, 0)),
            pl.BlockSpec((1, 1, cg, cg), lambda l, g: (l, g, 0, 0)),
        ],
        out_specs=pl.BlockSpec((1, 1, 2 * cg, cg), lambda l, g: (l, g, 0, 0)),
        out_shape=jax.ShapeDtypeStruct((depth, groups, 2 * cg, cg), BF16),
        name="fold_fourier",
    )(cs, w_fourier)


def _dft_tables(n):
    j = lax.broadcasted_iota(jnp.int32, (n, n), 0)
    k = lax.broadcasted_iota(jnp.int32, (n, n), 1)
    ang = ((j * k) % n).astype(F32) * (2.0 * math.pi / n)
    return jnp.cos(ang), jnp.sin(ang)


def _sincos_2d(length, dim):
    rows = length // GRID_W
    row = jnp.repeat(jnp.arange(rows, dtype=F32), GRID_W)
    col = jnp.tile(jnp.arange(GRID_W, dtype=F32), rows)
    quarter = dim // 4
    omega = jnp.power(POS_BASE, -jnp.arange(quarter, dtype=F32) / quarter)

    def axis_code(p):
        ang = p[:, None] * omega[None, :]
        return jnp.concatenate([jnp.sin(ang), jnp.cos(ang)], axis=-1)

    return jnp.concatenate([axis_code(row), axis_code(col)], axis=-1)


G_V, G_Q, G_KF, G_KB, G_GF, G_GB, G_GA, G_U, G_GATE_F = 0, 1, 2, 3, 4, 6, 8, 9, 10
HEADS_PER_GROUP = 8
N_SLAB_GROUPS = 11
INPROJ_SUBTILE = 256


def _inproj_kernel(add_pos, width, n_x_tiles, *refs):
    if add_pos:
        x_ref, ctx_ref, pos_ref, nw_ref, sc_ref, sh_ref, lbf_ref, lbb_ref, w_ref, o_ref = refs
    else:
        x_ref, ctx_ref, nw_ref, sc_ref, sh_ref, lbf_ref, lbb_ref, w_ref, o_ref = refs
    x = x_ref[0]
    if add_pos:
        x = x + pos_ref[...]
    x = jnp.where(pl.program_id(1) >= n_x_tiles, ctx_ref[0], x)
    ms = jnp.mean(x * x, axis=-1, keepdims=True)
    y = x * lax.rsqrt(ms + EPS) * nw_ref[0]
    h = (y * (1.0 + sc_ref[0]) + sh_ref[0]).astype(BF16)
    per = width // LANES
    sub = INPROJ_SUBTILE // LANES

    def project(j, t):
        lo = j * width + t * INPROJ_SUBTILE
        return _dot(h, w_ref[0, :, lo:lo + INPROJ_SUBTILE])

    def put(group, t, values):
        for s in range(sub):
            o_ref[0, group * per + t * sub + s] = values[:, s * LANES:(s + 1) * LANES].astype(BF16)

    def put_forget(k_group, g_group, t, z, lb_ref):
        one_minus_lb = 1.0 - lb_ref[0, :, t * INPROJ_SUBTILE:(t + 1) * INPROJ_SUBTILE]
        k = one_minus_lb * (1.0 / (1.0 + jnp.exp(z)))
        g = jnp.log(jnp.maximum(1.0 - k, F_MIN)) * LOG2_E
        hi = g.astype(BF16)
        lo = (g - hi.astype(F32)).astype(BF16)
        put(k_group, t, k)
        for s in range(sub):
            o_ref[0, g_group * per + 2 * (t * sub + s)] = hi[:, s * LANES:(s + 1) * LANES]
            o_ref[0, g_group * per + 2 * (t * sub + s) + 1] = lo[:, s * LANES:(s + 1) * LANES]

    for t in range(width // INPROJ_SUBTILE):
        put(G_V, t, project(0, t))
        put_forget(G_KF, G_GF, t, project(1, t), lbf_ref)
        put_forget(G_KB, G_GB, t, project(2, t), lbb_ref)
        put(G_Q, t, _silu(project(3, t)))
        put(G_GA, t, _silu(project(4, t)))
        put(G_U, t, project(5, t))
        put(G_GATE_F, t, _silu(project(6, t)))


def _inproj(layer, x, ctx, pos, norm_w, scale, shift, lbs, w_bf16):
    b, length, d = x.shape
    ctx_len = ctx.shape[1]
    depth, _, n = w_bf16.shape
    width = lbs.shape[2]
    assert n == 7 * width
    tm = min(256, ctx_len)
    assert length % tm == 0 and ctx_len % tm == 0
    nx, nc = length // tm, ctx_len // tm
    add_pos = pos is not None

    in_specs = [pl.BlockSpec((1, tm, d), lambda bi, i: (bi, jnp.minimum(i, nx - 1), 0)),
                pl.BlockSpec((1, tm, d), lambda bi, i: (bi, jnp.maximum(i - nx, 0), 0))]
    args = [x, ctx]
    if add_pos:
        in_specs.append(pl.BlockSpec((tm, d), lambda bi, i: (jnp.minimum(i, nx - 1), 0)))
        args.append(pos)
    mod_row = lambda bi, i: (bi + b * (i // nx), 0, 0)
    in_specs += [
        pl.BlockSpec((1, 1, d), lambda bi, i: (layer, 0, 0)),
        pl.BlockSpec((1, 1, d), mod_row),
        pl.BlockSpec((1, 1, d), mod_row),
        pl.BlockSpec((1, 1, width), lambda bi, i: (layer, 0, 0)),
        pl.BlockSpec((1, 1, width), lambda bi, i: (depth + layer, 0, 0)),
        pl.BlockSpec((1, d, n), lambda bi, i: (layer, 0, 0), pipeline_mode=pl.Buffered(1)),
    ]
    lb_rows = lbs.reshape(2 * depth, 1, width)
    args += [norm_w.reshape(depth, 1, d), scale.reshape(2 * b, 1, d), shift.reshape(2 * b, 1, d),
             lb_rows, lb_rows, w_bf16]
    n_slabs = N_SLAB_GROUPS * width // LANES
    return pl.pallas_call(
        functools.partial(_inproj_kernel, add_pos, width, nx),
        grid=(b, nx + nc),
        in_specs=in_specs,
        out_specs=pl.BlockSpec((1, n_slabs, tm, LANES), lambda bi, i: (bi, 0, i, 0)),
        out_shape=jax.ShapeDtypeStruct((b, n_slabs, length + ctx_len, LANES), BF16),
        compiler_params=pltpu.CompilerParams(
            dimension_semantics=("parallel", "parallel"), vmem_limit_bytes=VMEM_LIMIT),
        name="inproj",
    )(*args)


def _chunk_cumsum(tri, g_ref, rows):
    out = _dot(tri, jnp.concatenate([g_ref[0, 0, rows, :], g_ref[0, 1, rows, :]], axis=1))
    return out[:, :HEAD_DIM] + out[:, HEAD_DIM:]


def _half_level(q, k, bc, w, forward):
    c = q.shape[0]
    qs, ks = [], []
    zeros = jnp.zeros((w, HEAD_DIM), F32)
    for j in range(c // (2 * w)):
        lo = slice(2 * w * j, 2 * w * j + w)
        hi = slice(2 * w * j + w, 2 * w * (j + 1))
        if forward:
            ref = bc[2 * w * j + w - 1:2 * w * j + w, :]
            qs.append(q[hi] * jnp.exp2(bc[hi] - ref))
            ks += [k[lo] * jnp.exp2(ref - bc[lo]), zeros]
        else:
            ref = bc[2 * w * j + w:2 * w * j + w + 1, :]
            qs.append(q[lo] * jnp.exp2(bc[lo] - ref))
            ks += [zeros, k[hi] * jnp.exp2(ref - bc[hi])]
    qt = jnp.concatenate(qs, axis=0).astype(BF16)
    kt = jnp.concatenate(ks, axis=0).astype(BF16)
    return _dot_nt(qt, kt)


def _base_level(q, k, bc):
    c = q.shape[0]
    blocks = c // BASE_BLOCK
    mid = BASE_BLOCK // 2
    bc3 = bc.reshape(blocks, BASE_BLOCK, HEAD_DIM)
    e = bc3 - bc3[:, mid - 1:mid, :]
    qt = (q.reshape(blocks, BASE_BLOCK, HEAD_DIM) * jnp.exp2(e)).reshape(c, HEAD_DIM).astype(BF16)
    kt = (k.reshape(blocks, BASE_BLOCK, HEAD_DIM) * jnp.exp2(-e)).reshape(c, HEAD_DIM).astype(BF16)
    return _dot_nt(qt, kt)


def _level_products(q, kf, bcf, kb, bcb):
    c = q.shape[0]
    prods = {(0, True): _base_level(q, kf, bcf), (0, False): _base_level(q, kb, bcb)}
    w = BASE_BLOCK
    while w < c:
        prods[(w, True)] = _half_level(q, kf, bcf, w, True)
        prods[(w, False)] = _half_level(q, kb, bcb, w, False)
        w *= 2
    return prods


def _assemble_scores(prods, code_ref):
    c = prods[(0, True)].shape[0]
    nb = c // BASE_BLOCK
    blocks = []
    for r in range(nb):
        rows = slice(r * BASE_BLOCK, (r + 1) * BASE_BLOCK)
        code = code_ref[rows, :]
        f, b = prods[(0, True)][rows], prods[(0, False)][rows]
        blocks.append(jnp.where(code == 0, f + b, jnp.where(code == 1, f, jnp.where(code == -1, b, 0.0))))
    w = BASE_BLOCK
    level = 2
    while w < c:
        for forward, sign in ((True, 1), (False, -1)):
            r_w = prods[(w, forward)]
            for j in range(c // (2 * w)):
                first = 2 * w * j + (w if forward else 0)
                for r8 in range(w // BASE_BLOCK):
                    blk = first // BASE_BLOCK + r8
                    src = slice(j * w + r8 * BASE_BLOCK, j * w + (r8 + 1) * BASE_BLOCK)
                    code = code_ref[blk * BASE_BLOCK:(blk + 1) * BASE_BLOCK, :]
                    blocks[blk] = jnp.where(code == sign * level, r_w[src], blocks[blk])
        w *= 2
        level += 1
    return jnp.concatenate(blocks, axis=0)


def _scan_constants():
    c = SCAN_CHUNK
    t_idx = lax.broadcasted_iota(jnp.int32, (c, c), 0)
    s_idx = lax.broadcasted_iota(jnp.int32, (c, c), 1)
    tri_lo = (s_idx <= t_idx).astype(BF16)
    tri_up = (s_idx >= t_idx).astype(BF16)
    x = t_idx ^ s_idx
    lvl = jnp.ones((c, c), jnp.int32)
    w = BASE_BLOCK
    while w < c:
        lvl = lvl + (x >= w).astype(jnp.int32)
        w *= 2
    code = jnp.where(t_idx > s_idx, lvl, jnp.where(t_idx < s_idx, -lvl, 0))
    return tri_lo, tri_up, code


def _scan_kernel(n_x, n_c, v_ref, q_ref, kf_ref, kb_ref, gf_ref, gb_ref, ga_ref, hn_ref,
                 tlo_ref, tup_ref, code_ref, ya_ref, o_s, qt_s, u_s, d_s, st_s):
    c = SCAN_CHUNK
    hd = HEAD_DIM
    hn = hn_ref[0]
    tri_lo = tlo_ref[...]
    tri_up = tup_ref[...]

    def rows_of(i):
        return slice(i * c, (i + 1) * c)

    def gate_stage(i):
        rows = rows_of(i)
        return dict(i=i, rows=rows,
                    kf=kf_ref[0, 0, rows, :].astype(F32), kb=kb_ref[0, 0, rows, :].astype(F32),
                    q=q_ref[0, 0, rows, :].astype(F32),
                    bcf=_chunk_cumsum(tri_lo, gf_ref, rows), bcb=_chunk_cumsum(tri_up, gb_ref, rows))

    def decay_stage(s):
        q, kf, kb, bcf, bcb, rows = s["q"], s["kf"], s["kb"], s["bcf"], s["bcb"], s["rows"]
        s["prods"] = _level_products(q, kf, bcf, kb, bcb)
        qt_s[rows, :hd] = (q * jnp.exp2(bcf)).astype(BF16)
        qt_s[rows, hd:] = (q * jnp.exp2(bcb)).astype(BF16)
        ftot = bcf[c - 1:c, :]
        btot = bcb[0:1, :]
        khat = jnp.concatenate([kf * jnp.exp2(ftot - bcf), kb * jnp.exp2(btot - bcb)], axis=1)
        u_s[s["i"]] = _dot_tn(v_ref[0, 0, rows, :], khat.astype(BF16))
        d_s[s["i"]] = jnp.broadcast_to(jnp.concatenate([jnp.exp2(ftot), jnp.exp2(btot)], axis=1),
                                       (BASE_BLOCK, 2 * hd))

    def mix_stage(s):
        p = _assemble_scores(s["prods"], code_ref)
        o_s[s["rows"], :] = _dot(p.astype(BF16), v_ref[0, 0, s["rows"], :])

    order = list(range(n_x + n_c))
    groups = [order[g:g + CHUNK_GROUP] for g in range(0, len(order), CHUNK_GROUP)]
    for group in groups:
        chunks = [gate_stage(i) for i in group]
        for s in chunks:
            decay_stage(s)
        for s in chunks:
            mix_stage(s)

    ctx_chunks = list(range(n_x, n_x + n_c))
    x_chunks = list(range(n_x))
    state = jnp.zeros((hd, hd), F32)
    for i in ctx_chunks + x_chunks:
        st_s[i, :, :hd] = state.astype(BF16)
        state = state * d_s[i, 0:1, :hd] + u_s[i, :, :hd]
    state = jnp.zeros((hd, hd), F32)
    for i in ctx_chunks[::-1] + x_chunks[::-1]:
        st_s[i, :, hd:] = state.astype(BF16)
        state = state * d_s[i, 0:1, hd:] + u_s[i, :, hd:]

    for group in groups:
        inter = [(rows_of(i), _dot_nt(qt_s[rows_of(i), :], st_s[i])) for i in group]
        for rows, o_inter in inter:
            o = o_s[rows, :] + o_inter
            ms = jnp.mean(o * o, axis=-1, keepdims=True)
            y = o * lax.rsqrt(ms + EPS) * hn
            ya_ref[0, rows, :] = (y * ga_ref[0, 0, rows, :].astype(F32)).astype(BF16)


def _scan(layer, p, hn_w, n_ctx_rows, consts):
    b, _, total, _ = p.shape
    depth, hgrn_width = hn_w.shape
    heads = hgrn_width // HEAD_DIM
    c = SCAN_CHUNK
    n_chunks = total // c
    n_c = n_ctx_rows // c

    def slab(group):
        return pl.BlockSpec((1, 1, total, HEAD_DIM), lambda bi, h: (bi, h + group * heads, 0, 0))

    def slab_pair(group):
        return pl.BlockSpec((1, 2, total, HEAD_DIM), lambda bi, h: (bi, h + group * heads // 2, 0, 0))

    const = pl.BlockSpec((c, c), lambda bi, h: (0, 0))
    return pl.pallas_call(
        functools.partial(_scan_kernel, n_chunks - n_c, n_c),
        grid=(b, heads),
        in_specs=[slab(G_V), slab(G_Q), slab(G_KF), slab(G_KB), slab_pair(G_GF), slab_pair(G_GB),
                  slab(G_GA), pl.BlockSpec((1, 1, HEAD_DIM), lambda bi, h: (layer * heads + h, 0, 0)),
                  const, const, const],
        out_specs=pl.BlockSpec((1, total, HEAD_DIM), lambda bi, h: (bi, 0, h)),
        out_shape=jax.ShapeDtypeStruct((b, total, heads * HEAD_DIM), BF16),
        scratch_shapes=[
            pltpu.VMEM((total, HEAD_DIM), F32),
            pltpu.VMEM((total, 2 * HEAD_DIM), BF16),
            pltpu.VMEM((n_chunks, HEAD_DIM, 2 * HEAD_DIM), F32),
            pltpu.VMEM((n_chunks, BASE_BLOCK, 2 * HEAD_DIM), F32),
            pltpu.VMEM((n_chunks, HEAD_DIM, 2 * HEAD_DIM), BF16),
        ],
        compiler_params=pltpu.CompilerParams(
            dimension_semantics=("parallel", "parallel"), vmem_limit_bytes=VMEM_LIMIT),
        name="scan",
    )(p, p, p, p, p, p, p, hn_w.reshape(depth * heads, 1, HEAD_DIM), *consts)


def _fourier_kernel(length, u_ref, g_ref, ab_ref, bias_ref, cs_ref, y_ref, tt_s):
    cg = GROUP_DIM
    slabs = cg // LANES
    half = length // 2

    @pl.when(pl.program_id(1) == 0)
    def _():
        l_idx = lax.broadcasted_iota(jnp.int32, (half, half), 0)
        m_idx = lax.broadcasted_iota(jnp.int32, (half, half), 1)
        rev = ((l_idx + m_idx == half) & (l_idx >= 1)).astype(BF16)
        for g in range(FOURIER_GROUPS):
            cols = slice(g * cg, (g + 1) * cg)
            lower = jnp.concatenate([u_ref[0, slabs * g + s, :half, :] for s in range(slabs)], axis=1)
            upper = jnp.concatenate([u_ref[0, slabs * g + s, half:, :] for s in range(slabs)], axis=1)
            mirrored = _dot(rev, upper)
            lower = lower.astype(F32)
            tt_s[:half, cols] = _dot((lower + mirrored).astype(BF16), ab_ref[0, g, :cg, :]).astype(BF16)
            tt_s[half:, cols] = _dot((lower - mirrored).astype(BF16), ab_ref[0, g, cg:, :]).astype(BF16)
            tt_s[half:half + 1, cols] = _dot(upper[:16], ab_ref[0, g, :cg, :])[0:1].astype(BF16)

    acc = _dot(cs_ref[...], tt_s[...])
    for j in range(acc.shape[1] // LANES):
        cols = slice(j * LANES, (j + 1) * LANES)
        y = (acc[:, cols] + bias_ref[0, :, cols]) * g_ref[0, j].astype(F32)
        y_ref[0, :, cols] = y.astype(BF16)


def _fourier(layer, p, row0, length, ab, bias, cs):
    b = p.shape[0]
    depth, groups, _, cg = ab.shape
    width = groups * cg
    slabs = width // LANES
    tm = min(512, length)
    assert row0 % length == 0
    u_block = G_U * HEADS_PER_GROUP // slabs
    gate_block = G_GATE_F * HEADS_PER_GROUP // slabs
    return pl.pallas_call(
        functools.partial(_fourier_kernel, length),
        grid=(b, length // tm),
        in_specs=[
            pl.BlockSpec((1, slabs, length, LANES), lambda bi, i: (bi, u_block, row0 // length, 0)),
            pl.BlockSpec((1, slabs, tm, LANES), lambda bi, i: (bi, gate_block, row0 // tm + i, 0)),
            pl.BlockSpec((1, groups, 2 * cg, cg), lambda bi, i: (layer, 0, 0, 0)),
            pl.BlockSpec((1, 1, width), lambda bi, i: (layer, 0, 0)),
            pl.BlockSpec((tm, length), lambda bi, i: (i, 0)),
        ],
        out_specs=pl.BlockSpec((1, tm, width), lambda bi, i: (bi, i, 0)),
        out_shape=jax.ShapeDtypeStruct((b, length, width), BF16),
        scratch_shapes=[pltpu.VMEM((length, width), BF16)],
        compiler_params=pltpu.CompilerParams(
            dimension_semantics=("parallel", "arbitrary"), vmem_limit_bytes=VMEM_LIMIT),
        name="fourier",
    )(p, p, ab, bias.reshape(depth, 1, width), cs)


def _seq_dft_matrix(length):
    half = length // 2
    row = lax.broadcasted_iota(jnp.int32, (length, length), 0)
    col = lax.broadcasted_iota(jnp.int32, (length, length), 1)
    freq = jnp.where(col <= half, col, col - half)
    shift = jnp.where(col <= half, length // 4, half)
    ang = ((row * freq + shift) % length).astype(F32) * (2.0 * math.pi / length)
    return (jnp.sin(ang) * (1.0 / math.sqrt(length * GROUP_DIM))).astype(BF16)


def _outproj_kernel(add_pos, final_norm, *refs):
    refs = list(refs)
    ya_ref, yf_ref, w_ref, x_ref, gate_ref = refs[:5]
    rest = refs[5:]
    pos_ref = rest.pop(0) if add_pos else None
    fnw_ref = rest.pop(0) if final_norm else None
    o_ref = rest.pop(0)
    wa = ya_ref.shape[2]
    out = _dot(ya_ref[0], w_ref[0, :wa, :]) + _dot(yf_ref[0], w_ref[0, wa:, :])
    x = x_ref[0]
    if add_pos:
        x = x + pos_ref[...]
    x = x + gate_ref[0] * out
    if final_norm:
        ms = jnp.mean(x * x, axis=-1, keepdims=True)
        x = x * lax.rsqrt(ms + EPS) * fnw_ref[...]
    o_ref[0] = x


def _outproj(layer, y_a, row0, y_f, w_bf16, x, gate, pos, final_norm_w):
    b, length, d = x.shape
    wa, wf = y_a.shape[2], y_f.shape[2]
    tm = min(512, length)
    assert row0 % tm == 0
    add_pos = pos is not None
    final_norm = final_norm_w is not None
    in_specs = [
        pl.BlockSpec((1, tm, wa), lambda bi, i: (bi, row0 // tm + i, 0)),
        pl.BlockSpec((1, tm, wf), lambda bi, i: (bi, i, 0)),
        pl.BlockSpec((1, wa + wf, d), lambda bi, i: (layer, 0, 0)),
        pl.BlockSpec((1, tm, d), lambda bi, i: (bi, i, 0)),
        pl.BlockSpec((1, 1, d), lambda bi, i: (bi, 0, 0)),
    ]
    args = [y_a, y_f, w_bf16, x, gate.reshape(b, 1, d)]
    if add_pos:
        in_specs.append(pl.BlockSpec((tm, d), lambda bi, i: (i, 0)))
        args.append(pos)
    if final_norm:
        in_specs.append(pl.BlockSpec((1, d), lambda bi, i: (0, 0)))
        args.append(final_norm_w.reshape(1, d))
    return pl.pallas_call(
        functools.partial(_outproj_kernel, add_pos, final_norm),
        grid=(b, length // tm),
        in_specs=in_specs,
        out_specs=pl.BlockSpec((1, tm, d), lambda bi, i: (bi, i, 0)),
        out_shape=jax.ShapeDtypeStruct((b, length, d), F32),
        compiler_params=pltpu.CompilerParams(
            dimension_semantics=("parallel", "parallel"), vmem_limit_bytes=VMEM_LIMIT),
        name="outproj",
    )(*args)


def kernel(x, c, ctx, c_ctx, norm_w, w_ada, b_ada, w_in, lower_bounds, hgrn_norm_w,
           w_fourier, b_fourier, w_out, final_norm_w):
    b, length, d = x.shape
    depth = w_in.shape[0]
    ctx_len = ctx.shape[1]
    assert hgrn_norm_w.shape[1] == HEADS_PER_GROUP * HEAD_DIM

    pos = _sincos_2d(length, d)
    lbs = _lower_bounds(lower_bounds)
    rows = ((b + 1 + 7) // 8) * 8
    c_rows = jnp.zeros((rows, d), F32).at[:b].set(c).at[b].set(c_ctx)
    mod = _ada_all(c_rows, w_ada, b_ada)
    ab = _fold_fourier_weights(w_fourier)
    cs_x = _seq_dft_matrix(length)
    cs_c = _seq_dft_matrix(ctx_len)
    w_in_b = w_in.astype(BF16)
    w_out_b = w_out.astype(BF16)
    consts = _scan_constants()

    def both_streams(l, k):
        part = mod[l, :, k * d:(k + 1) * d]
        return jnp.concatenate([part[:b], jnp.broadcast_to(part[b:b + 1], (b, d))], axis=0)

    for l in range(depth):
        last = l == depth - 1
        shift, scale, gate = (both_streams(l, k) for k in range(3))
        p = _inproj(l, x, ctx, pos if l == 0 else None, norm_w, scale, shift, lbs, w_in_b)
        y_a = _scan(l, p, hgrn_norm_w, ctx_len, consts)
        yf_x = _fourier(l, p, 0, length, ab, b_fourier, cs_x)
        x = _outproj(l, y_a, 0, yf_x, w_out_b, x, gate[:b], pos if l == 0 else None,
                     final_norm_w if last else None)
        if not last:
            yf_c = _fourier(l, p, length, ctx_len, ab, b_fourier, cs_c)
            ctx = _outproj(l, y_a, length, yf_c, w_out_b, ctx, gate[b:], None, None)
    return x
```

```python
import functools
import math

import jax
import jax.numpy as jnp
from jax import lax
from jax.experimental import pallas as pl
from jax.experimental.pallas import tpu as pltpu

F32 = jnp.float32
BF16 = jnp.bfloat16

LANES = 128
HEAD_DIM = 128
GROUP_DIM = 256
FOURIER_GROUPS = 4
GRID_W = 64
POS_BASE = 10000.0
EPS = 1e-6
F_MIN = 1e-6
LOG2_E = 1.4426950408889634
SCAN_CHUNK = 128
BASE_BLOCK = 8
STAGE_SKEW = 3
VMEM_LIMIT = 56 * 1024 * 1024


def _sigmoid(z):
    return 1.0 / (1.0 + jnp.exp(-z))


def _silu(z):
    return z * _sigmoid(z)


def _dot(a, b):
    return jnp.dot(a, b, preferred_element_type=F32)


def _dot_nt(a, b):
    return lax.dot_general(a, b, (((1,), (1,)), ((), ())), preferred_element_type=F32)


def _dot_tn(a, b):
    return lax.dot_general(a, b, (((0,), (0,)), ((), ())), preferred_element_type=F32)


def _ada_kernel(c_ref, w_ref, b_ref, o_ref):
    sc = _silu(c_ref[...]).astype(BF16)
    o_ref[0] = _dot(sc, w_ref[0].astype(BF16)) + b_ref[0]


def _ada_all(c_rows, w_ada, b_ada):
    depth, d, n = w_ada.shape
    rows = c_rows.shape[0]
    tn = 1024
    return pl.pallas_call(
        _ada_kernel,
        grid=(depth, n // tn),
        in_specs=[
            pl.BlockSpec((rows, d), lambda l, j: (0, 0)),
            pl.BlockSpec((1, d, tn), lambda l, j: (l, 0, j)),
            pl.BlockSpec((1, 1, tn), lambda l, j: (l, 0, j)),
        ],
        out_specs=pl.BlockSpec((1, rows, tn), lambda l, j: (l, 0, j)),
        out_shape=jax.ShapeDtypeStruct((depth, rows, n), F32),
        name="ada_mod",
    )(c_rows, w_ada, b_ada.reshape(depth, 1, n))


def _lower_bound_kernel(depth, lb_ref, o_ref):
    for dr in range(2):
        rows = [lb_ref[dr * depth + l:dr * depth + l + 1, :] for l in range(depth)]
        mx = functools.reduce(jnp.maximum, rows)
        ex = [jnp.exp(r - mx) for r in rows]
        den = functools.reduce(lambda a, b: a + b, ex)
        p = [e / den for e in ex]
        run = p[0]
        for l in range(depth):
            if l > 0:
                run = run + p[l]
            o_ref[dr * depth + l:dr * depth + l + 1, :] = run - p[0]


def _lower_bounds(lower_bounds):
    two, depth, width = lower_bounds.shape
    out = pl.pallas_call(
        functools.partial(_lower_bound_kernel, depth),
        out_shape=jax.ShapeDtypeStruct((two * depth, width), F32),
        name="lower_bounds",
    )(lower_bounds.reshape(two * depth, width))
    return out.reshape(two, depth, width)


def _fold_kernel(cs_ref, w_ref, o_ref):
    o_ref[0, 0] = jnp.dot(cs_ref[...], w_ref[0, 0], preferred_element_type=F32,
                          precision=lax.Precision.HIGHEST).astype(BF16)


def _fold_fourier_weights(w_fourier):
    depth, groups, cg, _ = w_fourier.shape
    cs = jnp.concatenate(_dft_tables(cg), axis=0)
    return pl.pallas_call(
        _fold_kernel,
        grid=(depth, groups),
        in_specs=[
            pl.BlockSpec((2 * cg, cg), lambda l, g: (0, 0)),
            pl.BlockSpec((1, 1, cg, cg), lambda l, g: (l, g, 0, 0)),
        ],
        out_specs=pl.BlockSpec((1, 1, 2 * cg, cg), lambda l, g: (l, g, 0, 0)),
        out_shape=jax.ShapeDtypeStruct((depth, groups, 2 * cg, cg), BF16),
        name="fold_fourier",
    )(cs, w_fourier)


def _dft_tables(n):
    j = lax.broadcasted_iota(jnp.int32, (n, n), 0)
    k = lax.broadcasted_iota(jnp.int32, (n, n), 1)
    ang = ((j * k) % n).astype(F32) * (2.0 * math.pi / n)
    return jnp.cos(ang), jnp.sin(ang)


def _sincos_2d(length, dim):
    rows = length // GRID_W
    row = jnp.repeat(jnp.arange(rows, dtype=F32), GRID_W)
    col = jnp.tile(jnp.arange(GRID_W, dtype=F32), rows)
    quarter = dim // 4
    omega = jnp.power(POS_BASE, -jnp.arange(quarter, dtype=F32) / quarter)

    def axis_code(p):
        ang = p[:, None] * omega[None, :]
        return jnp.concatenate([jnp.sin(ang), jnp.cos(ang)], axis=-1)

    return jnp.concatenate([axis_code(row), axis_code(col)], axis=-1)


G_V, G_Q, G_KF, G_KB, G_GF, G_GB, G_GA, G_U, G_GATE_F = 0, 1, 2, 3, 4, 6, 8, 9, 10
HEADS_PER_GROUP = 8
N_SLAB_GROUPS = 11
INPROJ_SUBTILE = 256


def _inproj_kernel(add_pos, width, n_x_tiles, *refs):
    if add_pos:
        x_ref, ctx_ref, pos_ref, nw_ref, sc_ref, sh_ref, lbf_ref, lbb_ref, w_ref, o_ref = refs
    else:
        x_ref, ctx_ref, nw_ref, sc_ref, sh_ref, lbf_ref, lbb_ref, w_ref, o_ref = refs
    x = x_ref[0]
    if add_pos:
        x = x + pos_ref[...]
    x = jnp.where(pl.program_id(1) >= n_x_tiles, ctx_ref[0], x)
    ms = jnp.mean(x * x, axis=-1, keepdims=True)
    y = x * lax.rsqrt(ms + EPS) * nw_ref[0]
    h = (y * (1.0 + sc_ref[0]) + sh_ref[0]).astype(BF16)
    per = width // LANES
    sub = INPROJ_SUBTILE // LANES

    def project(j, t):
        lo = j * width + t * INPROJ_SUBTILE
        return _dot(h, w_ref[0, :, lo:lo + INPROJ_SUBTILE])

    def put(group, t, values):
        for s in range(sub):
            o_ref[0, group * per + t * sub + s] = values[:, s * LANES:(s + 1) * LANES].astype(BF16)

    def put_forget(k_group, g_group, t, z, lb_ref):
        one_minus_lb = 1.0 - lb_ref[0, :, t * INPROJ_SUBTILE:(t + 1) * INPROJ_SUBTILE]
        k = one_minus_lb * (1.0 / (1.0 + jnp.exp(z)))
        g = jnp.log(jnp.maximum(1.0 - k, F_MIN)) * LOG2_E
        hi = g.astype(BF16)
        lo = (g - hi.astype(F32)).astype(BF16)
        put(k_group, t, k)
        for s in range(sub):
            o_ref[0, g_group * per + 2 * (t * sub + s)] = hi[:, s * LANES:(s + 1) * LANES]
            o_ref[0, g_group * per + 2 * (t * sub + s) + 1] = lo[:, s * LANES:(s + 1) * LANES]

    for t in range(width // INPROJ_SUBTILE):
        put(G_V, t, project(0, t))
        put_forget(G_KF, G_GF, t, project(1, t), lbf_ref)
        put_forget(G_KB, G_GB, t, project(2, t), lbb_ref)
        put(G_Q, t, _silu(project(3, t)))
        put(G_GA, t, _silu(project(4, t)))
        put(G_U, t, project(5, t))
        put(G_GATE_F, t, _silu(project(6, t)))


def _inproj(layer, x, ctx, pos, norm_w, scale, shift, lbs, w_bf16):
    b, length, d = x.shape
    ctx_len = ctx.shape[1]
    depth, _, n = w_bf16.shape
    width = lbs.shape[2]
    assert n == 7 * width
    tm = min(256, ctx_len)
    assert length % tm == 0 and ctx_len % tm == 0
    nx, nc = length // tm, ctx_len // tm
    add_pos = pos is not None

    in_specs = [pl.BlockSpec((1, tm, d), lambda bi, i: (bi, jnp.minimum(i, nx - 1), 0)),
                pl.BlockSpec((1, tm, d), lambda bi, i: (bi, jnp.maximum(i - nx, 0), 0))]
    args = [x, ctx]
    if add_pos:
        in_specs.append(pl.BlockSpec((tm, d), lambda bi, i: (jnp.minimum(i, nx - 1), 0)))
        args.append(pos)
    mod_row = lambda bi, i: (bi + b * (i // nx), 0, 0)
    in_specs += [
        pl.BlockSpec((1, 1, d), lambda bi, i: (layer, 0, 0)),
        pl.BlockSpec((1, 1, d), mod_row),
        pl.BlockSpec((1, 1, d), mod_row),
        pl.BlockSpec((1, 1, width), lambda bi, i: (layer, 0, 0)),
        pl.BlockSpec((1, 1, width), lambda bi, i: (depth + layer, 0, 0)),
        pl.BlockSpec((1, d, n), lambda bi, i: (layer, 0, 0), pipeline_mode=pl.Buffered(1)),
    ]
    lb_rows = lbs.reshape(2 * depth, 1, width)
    args += [norm_w.reshape(depth, 1, d), scale.reshape(2 * b, 1, d), shift.reshape(2 * b, 1, d),
             lb_rows, lb_rows, w_bf16]
    n_slabs = N_SLAB_GROUPS * width // LANES
    return pl.pallas_call(
        functools.partial(_inproj_kernel, add_pos, width, nx),
        grid=(b, nx + nc),
        in_specs=in_specs,
        out_specs=pl.BlockSpec((1, n_slabs, tm, LANES), lambda bi, i: (bi, 0, i, 0)),
        out_shape=jax.ShapeDtypeStruct((b, n_slabs, length + ctx_len, LANES), BF16),
        compiler_params=pltpu.CompilerParams(
            dimension_semantics=("parallel", "parallel"), vmem_limit_bytes=VMEM_LIMIT),
        name="inproj",
    )(*args)


def _chunk_cumsum(tri, g_ref, rows):
    out = _dot(tri, jnp.concatenate([g_ref[0, 0, rows, :], g_ref[0, 1, rows, :]], axis=1))
    return out[:, :HEAD_DIM] + out[:, HEAD_DIM:]


def _half_level(q, k, bc, w, forward):
    c = q.shape[0]
    qs, ks = [], []
    zeros = jnp.zeros((w, HEAD_DIM), F32)
    for j in range(c // (2 * w)):
        lo = slice(2 * w * j, 2 * w * j + w)
        hi = slice(2 * w * j + w, 2 * w * (j + 1))
        if forward:
            ref = bc[2 * w * j + w - 1:2 * w * j + w, :]
            qs.append(q[hi] * jnp.exp2(bc[hi] - ref))
            ks += [k[lo] * jnp.exp2(ref - bc[lo]), zeros]
        else:
            ref = bc[2 * w * j + w:2 * w * j + w + 1, :]
            qs.append(q[lo] * jnp.exp2(bc[lo] - ref))
            ks += [zeros, k[hi] * jnp.exp2(ref - bc[hi])]
    qt = jnp.concatenate(qs, axis=0).astype(BF16)
    kt = jnp.concatenate(ks, axis=0).astype(BF16)
    return _dot_nt(qt, kt)


def _base_level(q, k, bc):
    c = q.shape[0]
    blocks = c // BASE_BLOCK
    mid = BASE_BLOCK // 2
    bc3 = bc.reshape(blocks, BASE_BLOCK, HEAD_DIM)
    e = bc3 - bc3[:, mid - 1:mid, :]
    qt = (q.reshape(blocks, BASE_BLOCK, HEAD_DIM) * jnp.exp2(e)).reshape(c, HEAD_DIM).astype(BF16)
    kt = (k.reshape(blocks, BASE_BLOCK, HEAD_DIM) * jnp.exp2(-e)).reshape(c, HEAD_DIM).astype(BF16)
    return _dot_nt(qt, kt)


def _level_products(q, kf, bcf, kb, bcb):
    c = q.shape[0]
    prods = {(0, True): _base_level(q, kf, bcf), (0, False): _base_level(q, kb, bcb)}
    w = BASE_BLOCK
    while w < c:
        prods[(w, True)] = _half_level(q, kf, bcf, w, True)
        prods[(w, False)] = _half_level(q, kb, bcb, w, False)
        w *= 2
    return prods


def _assemble_scores(prods, code_ref):
    c = prods[(0, True)].shape[0]
    nb = c // BASE_BLOCK
    blocks = []
    for r in range(nb):
        rows = slice(r * BASE_BLOCK, (r + 1) * BASE_BLOCK)
        code = code_ref[rows, :]
        f, b = prods[(0, True)][rows], prods[(0, False)][rows]
        blocks.append(jnp.where(code == 0, f + b, jnp.where(code == 1, f, jnp.where(code == -1, b, 0.0))))
    w = BASE_BLOCK
    level = 2
    while w < c:
        for forward, sign in ((True, 1), (False, -1)):
            r_w = prods[(w, forward)]
            for j in range(c // (2 * w)):
                first = 2 * w * j + (w if forward else 0)
                for r8 in range(w // BASE_BLOCK):
                    blk = first // BASE_BLOCK + r8
                    src = slice(j * w + r8 * BASE_BLOCK, j * w + (r8 + 1) * BASE_BLOCK)
                    code = code_ref[blk * BASE_BLOCK:(blk + 1) * BASE_BLOCK, :]
                    blocks[blk] = jnp.where(code == sign * level, r_w[src], blocks[blk])
        w *= 2
        level += 1
    return jnp.concatenate(blocks, axis=0)


def _scan_constants():
    c = SCAN_CHUNK
    t_idx = lax.broadcasted_iota(jnp.int32, (c, c), 0)
    s_idx = lax.broadcasted_iota(jnp.int32, (c, c), 1)
    tri_lo = (s_idx <= t_idx).astype(BF16)
    tri_up = (s_idx >= t_idx).astype(BF16)
    x = t_idx ^ s_idx
    lvl = jnp.ones((c, c), jnp.int32)
    w = BASE_BLOCK
    while w < c:
        lvl = lvl + (x >= w).astype(jnp.int32)
        w *= 2
    code = jnp.where(t_idx > s_idx, lvl, jnp.where(t_idx < s_idx, -lvl, 0))
    return tri_lo, tri_up, code


def _scan_kernel(n_x, n_c, v_ref, q_ref, kf_ref, kb_ref, gf_ref, gb_ref, ga_ref, hn_ref,
                 tlo_ref, tup_ref, code_ref, ya_ref, o_s, qt_s, u_s, d_s, st_s):
    c = SCAN_CHUNK
    hd = HEAD_DIM
    hn = hn_ref[0]
    tri_lo = tlo_ref[...]
    tri_up = tup_ref[...]

    def rows_of(i):
        return slice(i * c, (i + 1) * c)

    def gate_stage(i):
        rows = rows_of(i)
        return dict(i=i, rows=rows,
                    kf=kf_ref[0, 0, rows, :].astype(F32), kb=kb_ref[0, 0, rows, :].astype(F32),
                    q=q_ref[0, 0, rows, :].astype(F32),
                    bcf=_chunk_cumsum(tri_lo, gf_ref, rows), bcb=_chunk_cumsum(tri_up, gb_ref, rows))

    def decay_stage(s):
        q, kf, kb, bcf, bcb, rows = s["q"], s["kf"], s["kb"], s["bcf"], s["bcb"], s["rows"]
        s["prods"] = _level_products(q, kf, bcf, kb, bcb)
        qt_s[rows, :hd] = (q * jnp.exp2(bcf)).astype(BF16)
        qt_s[rows, hd:] = (q * jnp.exp2(bcb)).astype(BF16)
        ftot = bcf[c - 1:c, :]
        btot = bcb[0:1, :]
        khat = jnp.concatenate([kf * jnp.exp2(ftot - bcf), kb * jnp.exp2(btot - bcb)], axis=1)
        u_s[s["i"]] = _dot_tn(v_ref[0, 0, rows, :], khat.astype(BF16))
        d_s[s["i"]] = jnp.broadcast_to(jnp.concatenate([jnp.exp2(ftot), jnp.exp2(btot)], axis=1),
                                       (BASE_BLOCK, 2 * hd))

    def mix_stage(s):
        p = _assemble_scores(s["prods"], code_ref)
        o_s[s["rows"], :] = _dot(p.astype(BF16), v_ref[0, 0, s["rows"], :])

    n = n_x + n_c
    staged = {}
    for k in range(-STAGE_SKEW, n + STAGE_SKEW):
        if 0 <= k < n:
            decay_stage(staged[k])
        if 0 <= k + STAGE_SKEW < n:
            staged[k + STAGE_SKEW] = gate_stage(k + STAGE_SKEW)
        if 0 <= k - STAGE_SKEW < n:
            mix_stage(staged.pop(k - STAGE_SKEW))

    ctx_chunks = list(range(n_x, n_x + n_c))
    x_chunks = list(range(n_x))
    state = jnp.zeros((hd, hd), F32)
    for i in ctx_chunks + x_chunks:
        st_s[i, :, :hd] = state.astype(BF16)
        state = state * d_s[i, 0:1, :hd] + u_s[i, :, :hd]
    state = jnp.zeros((hd, hd), F32)
    for i in ctx_chunks[::-1] + x_chunks[::-1]:
        st_s[i, :, hd:] = state.astype(BF16)
        state = state * d_s[i, 0:1, hd:] + u_s[i, :, hd:]

    inter = {}
    for k in range(-STAGE_SKEW, n):
        if k + STAGE_SKEW < n:
            i = k + STAGE_SKEW
            inter[i] = _dot_nt(qt_s[rows_of(i), :], st_s[i])
        if k >= 0:
            rows = rows_of(k)
            o = o_s[rows, :] + inter.pop(k)
            ms = jnp.mean(o * o, axis=-1, keepdims=True)
            y = o * lax.rsqrt(ms + EPS) * hn
            ya_ref[0, rows, :] = (y * ga_ref[0, 0, rows, :].astype(F32)).astype(BF16)


def _scan(layer, p, hn_w, n_ctx_rows, consts):
    b, _, total, _ = p.shape
    depth, hgrn_width = hn_w.shape
    heads = hgrn_width // HEAD_DIM
    c = SCAN_CHUNK
    n_chunks = total // c
    n_c = n_ctx_rows // c

    def slab(group):
        return pl.BlockSpec((1, 1, total, HEAD_DIM), lambda bi, h: (bi, h + group * heads, 0, 0))

    def slab_pair(group):
        return pl.BlockSpec((1, 2, total, HEAD_DIM), lambda bi, h: (bi, h + group * heads // 2, 0, 0))

    const = pl.BlockSpec((c, c), lambda bi, h: (0, 0))
    return pl.pallas_call(
        functools.partial(_scan_kernel, n_chunks - n_c, n_c),
        grid=(b, heads),
        in_specs=[slab(G_V), slab(G_Q), slab(G_KF), slab(G_KB), slab_pair(G_GF), slab_pair(G_GB),
                  slab(G_GA), pl.BlockSpec((1, 1, HEAD_DIM), lambda bi, h: (layer * heads + h, 0, 0)),
                  const, const, const],
        out_specs=pl.BlockSpec((1, total, HEAD_DIM), lambda bi, h: (bi, 0, h)),
        out_shape=jax.ShapeDtypeStruct((b, total, heads * HEAD_DIM), BF16),
        scratch_shapes=[
            pltpu.VMEM((total, HEAD_DIM), F32),
            pltpu.VMEM((total, 2 * HEAD_DIM), BF16),
            pltpu.VMEM((n_chunks, HEAD_DIM, 2 * HEAD_DIM), F32),
            pltpu.VMEM((n_chunks, BASE_BLOCK, 2 * HEAD_DIM), F32),
            pltpu.VMEM((n_chunks, HEAD_DIM, 2 * HEAD_DIM), BF16),
        ],
        compiler_params=pltpu.CompilerParams(
            dimension_semantics=("parallel", "parallel"), vmem_limit_bytes=VMEM_LIMIT),
        name="scan",
    )(p, p, p, p, p, p, p, hn_w.reshape(depth * heads, 1, HEAD_DIM), *consts)


def _fourier_out_kernel(length, add_pos, final_norm, *refs):
    refs = list(refs)
    u_ref, g_ref, ab_ref, bias_ref, cs_ref, ya_ref, w_ref, x_ref, gate_ref = refs[:9]
    rest = refs[9:]
    pos_ref = rest.pop(0) if add_pos else None
    fnw_ref = rest.pop(0) if final_norm else None
    o_ref, tt_s = rest
    cg = GROUP_DIM
    slabs = cg // LANES
    half = length // 2

    @pl.when(pl.program_id(1) == 0)
    def _():
        l_idx = lax.broadcasted_iota(jnp.int32, (half, half), 0)
        m_idx = lax.broadcasted_iota(jnp.int32, (half, half), 1)
        rev = ((l_idx + m_idx == half) & (l_idx >= 1)).astype(BF16)
        for g in range(FOURIER_GROUPS):
            cols = slice(g * cg, (g + 1) * cg)
            lower = jnp.concatenate([u_ref[0, slabs * g + s, :half, :] for s in range(slabs)], axis=1)
            upper = jnp.concatenate([u_ref[0, slabs * g + s, half:, :] for s in range(slabs)], axis=1)
            mirrored = _dot(rev, upper)
            lower = lower.astype(F32)
            tt_s[:half, cols] = _dot((lower + mirrored).astype(BF16), ab_ref[0, g, :cg, :]).astype(BF16)
            tt_s[half:, cols] = _dot((lower - mirrored).astype(BF16), ab_ref[0, g, cg:, :]).astype(BF16)
            tt_s[half:half + 1, cols] = _dot(upper[:16], ab_ref[0, g, :cg, :])[0:1].astype(BF16)

    acc = _dot(cs_ref[...], tt_s[...])
    gate_f = jnp.concatenate([g_ref[0, j] for j in range(acc.shape[1] // LANES)], axis=1)
    y_f = ((acc + bias_ref[0]) * gate_f.astype(F32)).astype(BF16)
    wa = ya_ref.shape[2]
    out = _dot(ya_ref[0], w_ref[0, :wa, :]) + _dot(y_f, w_ref[0, wa:, :])
    x = x_ref[0]
    if add_pos:
        x = x + pos_ref[...]
    x = x + gate_ref[0] * out
    if final_norm:
        ms = jnp.mean(x * x, axis=-1, keepdims=True)
        x = x * lax.rsqrt(ms + EPS) * fnw_ref[...]
    o_ref[0] = x


def _fourier_out(layer, p, y_a, row0, ab, bias, cs, w_bf16, x, gate, pos, final_norm_w):
    b, length, d = x.shape
    depth, groups, _, cg = ab.shape
    width = groups * cg
    wa = y_a.shape[2]
    slabs = width // LANES
    tm = min(512, length)
    assert row0 % length == 0
    u_block = G_U * HEADS_PER_GROUP // slabs
    gate_block = G_GATE_F * HEADS_PER_GROUP // slabs
    add_pos = pos is not None
    final_norm = final_norm_w is not None
    in_specs = [
        pl.BlockSpec((1, slabs, length, LANES), lambda bi, i: (bi, u_block, row0 // length, 0)),
        pl.BlockSpec((1, slabs, tm, LANES), lambda bi, i: (bi, gate_block, row0 // tm + i, 0)),
        pl.BlockSpec((1, groups, 2 * cg, cg), lambda bi, i: (layer, 0, 0, 0)),
        pl.BlockSpec((1, 1, width), lambda bi, i: (layer, 0, 0)),
        pl.BlockSpec((tm, length), lambda bi, i: (i, 0)),
        pl.BlockSpec((1, tm, wa), lambda bi, i: (bi, row0 // tm + i, 0)),
        pl.BlockSpec((1, wa + width, d), lambda bi, i: (layer, 0, 0)),
        pl.BlockSpec((1, tm, d), lambda bi, i: (bi, i, 0)),
        pl.BlockSpec((1, 1, d), lambda bi, i: (bi, 0, 0)),
    ]
    args = [p, p, ab, bias.reshape(depth, 1, width), cs, y_a, w_bf16, x, gate.reshape(b, 1, d)]
    if add_pos:
        in_specs.append(pl.BlockSpec((tm, d), lambda bi, i: (i, 0)))
        args.append(pos)
    if final_norm:
        in_specs.append(pl.BlockSpec((1, d), lambda bi, i: (0, 0)))
        args.append(final_norm_w.reshape(1, d))
    return pl.pallas_call(
        functools.partial(_fourier_out_kernel, length, add_pos, final_norm),
        grid=(b, length // tm),
        in_specs=in_specs,
        out_specs=pl.BlockSpec((1, tm, d), lambda bi, i: (bi, i, 0)),
        out_shape=jax.ShapeDtypeStruct((b, length, d), F32),
        scratch_shapes=[pltpu.VMEM((length, width), BF16)],
        compiler_params=pltpu.CompilerParams(
            dimension_semantics=("parallel", "arbitrary"), vmem_limit_bytes=VMEM_LIMIT),
        name="fourier_out",
    )(*args)


def _seq_dft_matrix(length):
    half = length // 2
    row = lax.broadcasted_iota(jnp.int32, (length, length), 0)
    col = lax.broadcasted_iota(jnp.int32, (length, length), 1)
    freq = jnp.where(col <= half, col, col - half)
    shift = jnp.where(col <= half, length // 4, half)
    ang = ((row * freq + shift) % length).astype(F32) * (2.0 * math.pi / length)
    return (jnp.sin(ang) * (1.0 / math.sqrt(length * GROUP_DIM))).astype(BF16)


def kernel(x, c, ctx, c_ctx, norm_w, w_ada, b_ada, w_in, lower_bounds, hgrn_norm_w,
           w_fourier, b_fourier, w_out, final_norm_w):
    b, length, d = x.shape
    depth = w_in.shape[0]
    ctx_len = ctx.shape[1]
    assert hgrn_norm_w.shape[1] == HEADS_PER_GROUP * HEAD_DIM

    pos = _sincos_2d(length, d)
    lbs = _lower_bounds(lower_bounds)
    rows = ((b + 1 + 7) // 8) * 8
    c_rows = jnp.zeros((rows, d), F32).at[:b].set(c).at[b].set(c_ctx)
    mod = _ada_all(c_rows, w_ada, b_ada)
    ab = _fold_fourier_weights(w_fourier)
    cs_x = _seq_dft_matrix(length)
    cs_c = _seq_dft_matrix(ctx_len)
    w_in_b = w_in.astype(BF16)
    w_out_b = w_out.astype(BF16)
    consts = _scan_constants()

    def both_streams(l, k):
        part = mod[l, :, k * d:(k + 1) * d]
        return jnp.concatenate([part[:b], jnp.broadcast_to(part[b:b + 1], (b, d))], axis=0)

    for l in range(depth):
        last = l == depth - 1
        shift, scale, gate = (both_streams(l, k) for k in range(3))
        p = _inproj(l, x, ctx, pos if l == 0 else None, norm_w, scale, shift, lbs, w_in_b)
        y_a = _scan(l, p, hgrn_norm_w, ctx_len, consts)
        x_next = _fourier_out(l, p, y_a, 0, ab, b_fourier, cs_x, w_out_b, x, gate[:b],
                              pos if l == 0 else None, final_norm_w if last else None)
        if not last:
            ctx = _fourier_out(l, p, y_a, length, ab, b_fourier, cs_c, w_out_b, ctx, gate[b:], None, None)
        x = x_next
    return x
```

```python
import functools
import math

import jax
import jax.numpy as jnp
from jax import lax
from jax.experimental import pallas as pl
from jax.experimental.pallas import tpu as pltpu

F32 = jnp.float32
BF16 = jnp.bfloat16

LANES = 128
HEAD_DIM = 128
GROUP_DIM = 256
FOURIER_GROUPS = 4
GRID_W = 64
POS_BASE = 10000.0
EPS = 1e-6
F_MIN = 1e-6
LOG2_E = 1.4426950408889634
SCAN_CHUNK = 128
BASE_BLOCK = 8
STAGE_SKEW = 4
VMEM_LIMIT = 56 * 1024 * 1024


def _sigmoid(z):
    return 1.0 / (1.0 + jnp.exp(-z))


def _silu(z):
    return z * _sigmoid(z)


def _dot(a, b):
    return jnp.dot(a, b, preferred_element_type=F32)


def _dot_nt(a, b):
    return lax.dot_general(a, b, (((1,), (1,)), ((), ())), preferred_element_type=F32)


def _dot_tn(a, b):
    return lax.dot_general(a, b, (((0,), (0,)), ((), ())), preferred_element_type=F32)


def _ada_kernel(c_ref, w_ref, b_ref, o_ref):
    sc = _silu(c_ref[...]).astype(BF16)
    o_ref[0] = _dot(sc, w_ref[0].astype(BF16)) + b_ref[0]


def _ada_all(c_rows, w_ada, b_ada):
    depth, d, n = w_ada.shape
    rows = c_rows.shape[0]
    tn = 1024
    return pl.pallas_call(
        _ada_kernel,
        grid=(depth, n // tn),
        in_specs=[
            pl.BlockSpec((rows, d), lambda l, j: (0, 0)),
            pl.BlockSpec((1, d, tn), lambda l, j: (l, 0, j)),
            pl.BlockSpec((1, 1, tn), lambda l, j: (l, 0, j)),
        ],
        out_specs=pl.BlockSpec((1, rows, tn), lambda l, j: (l, 0, j)),
        out_shape=jax.ShapeDtypeStruct((depth, rows, n), F32),
        name="ada_mod",
    )(c_rows, w_ada, b_ada.reshape(depth, 1, n))


def _lower_bound_kernel(depth, lb_ref, o_ref):
    for dr in range(2):
        rows = [lb_ref[dr * depth + l:dr * depth + l + 1, :] for l in range(depth)]
        mx = functools.reduce(jnp.maximum, rows)
        ex = [jnp.exp(r - mx) for r in rows]
        den = functools.reduce(lambda a, b: a + b, ex)
        p = [e / den for e in ex]
        run = p[0]
        for l in range(depth):
            if l > 0:
                run = run + p[l]
            o_ref[dr * depth + l:dr * depth + l + 1, :] = run - p[0]


def _lower_bounds(lower_bounds):
    two, depth, width = lower_bounds.shape
    out = pl.pallas_call(
        functools.partial(_lower_bound_kernel, depth),
        out_shape=jax.ShapeDtypeStruct((two * depth, width), F32),
        name="lower_bounds",
    )(lower_bounds.reshape(two * depth, width))
    return out.reshape(two, depth, width)


def _fold_kernel(cs_ref, w_ref, o_ref):
    o_ref[0, 0] = jnp.dot(cs_ref[...], w_ref[0, 0], preferred_element_type=F32,
                          precision=lax.Precision.HIGHEST).astype(BF16)


def _fold_fourier_weights(w_fourier):
    depth, groups, cg, _ = w_fourier.shape
    cs = jnp.concatenate(_dft_tables(cg), axis=0)
    return pl.pallas_call(
        _fold_kernel,
        grid=(depth, groups),
        in_specs=[
            pl.BlockSpec((2 * cg, cg), lambda l, g: (0, 0)),
            pl.BlockSpec((1, 1, cg, cg), lambda l, g: (l, g, 0, 0)),
        ],
        out_specs=pl.BlockSpec((1, 1, 2 * cg, cg), lambda l, g: (l, g, 0, 0)),
        out_shape=jax.ShapeDtypeStruct((depth, groups, 2 * cg, cg), BF16),
        name="fold_fourier",
    )(cs, w_fourier)


def _dft_tables(n):
    j = lax.broadcasted_iota(jnp.int32, (n, n), 0)
    k = lax.broadcasted_iota(jnp.int32, (n, n), 1)
    ang = ((j * k) % n).astype(F32) * (2.0 * math.pi / n)
    return jnp.cos(ang), jnp.sin(ang)


def _sincos_2d(length, dim):
    rows = length // GRID_W
    row = jnp.repeat(jnp.arange(rows, dtype=F32), GRID_W)
    col = jnp.tile(jnp.arange(GRID_W, dtype=F32), rows)
    quarter = dim // 4
    omega = jnp.power(POS_BASE, -jnp.arange(quarter, dtype=F32) / quarter)

    def axis_code(p):
        ang = p[:, None] * omega[None, :]
        return jnp.concatenate([jnp.sin(ang), jnp.cos(ang)], axis=-1)

    return jnp.concatenate([axis_code(row), axis_code(col)], axis=-1)


G_V, G_Q, G_KF, G_KB, G_GF, G_GB, G_GA, G_U, G_GATE_F = 0, 1, 2, 3, 4, 6, 8, 9, 10
HEADS_PER_GROUP = 8
N_SLAB_GROUPS = 11
INPROJ_SUBTILE = 256


def _inproj_kernel(add_pos, width, n_x_tiles, *refs):
    if add_pos:
        x_ref, ctx_ref, pos_ref, nw_ref, sc_ref, sh_ref, lbf_ref, lbb_ref, w_ref, o_ref = refs
    else:
        x_ref, ctx_ref, nw_ref, sc_ref, sh_ref, lbf_ref, lbb_ref, w_ref, o_ref = refs
    x = x_ref[0]
    if add_pos:
        x = x + pos_ref[...]
    x = jnp.where(pl.program_id(1) >= n_x_tiles, ctx_ref[0], x)
    ms = jnp.mean(x * x, axis=-1, keepdims=True)
    y = x * lax.rsqrt(ms + EPS) * nw_ref[0]
    h = (y * (1.0 + sc_ref[0]) + sh_ref[0]).astype(BF16)
    per = width // LANES
    sub = INPROJ_SUBTILE // LANES

    def project(j, t):
        lo = j * width + t * INPROJ_SUBTILE
        return _dot(h, w_ref[0, :, lo:lo + INPROJ_SUBTILE])

    def put(group, t, values):
        for s in range(sub):
            o_ref[0, group * per + t * sub + s] = values[:, s * LANES:(s + 1) * LANES].astype(BF16)

    def put_forget(k_group, g_group, t, z, lb_ref):
        one_minus_lb = 1.0 - lb_ref[0, :, t * INPROJ_SUBTILE:(t + 1) * INPROJ_SUBTILE]
        k = one_minus_lb * (1.0 / (1.0 + jnp.exp(z)))
        g = jnp.log(jnp.maximum(1.0 - k, F_MIN)) * LOG2_E
        hi = g.astype(BF16)
        lo = (g - hi.astype(F32)).astype(BF16)
        put(k_group, t, k)
        for s in range(sub):
            o_ref[0, g_group * per + 2 * (t * sub + s)] = hi[:, s * LANES:(s + 1) * LANES]
            o_ref[0, g_group * per + 2 * (t * sub + s) + 1] = lo[:, s * LANES:(s + 1) * LANES]

    for t in range(width // INPROJ_SUBTILE):
        put(G_V, t, project(0, t))
        put_forget(G_KF, G_GF, t, project(1, t), lbf_ref)
        put_forget(G_KB, G_GB, t, project(2, t), lbb_ref)
        put(G_Q, t, _silu(project(3, t)))
        put(G_GA, t, _silu(project(4, t)))
        put(G_U, t, project(5, t))
        put(G_GATE_F, t, _silu(project(6, t)))


def _inproj(layer, x, ctx, pos, norm_w, scale, shift, lbs, w_bf16):
    b, length, d = x.shape
    ctx_len = ctx.shape[1]
    depth, _, n = w_bf16.shape
    width = lbs.shape[2]
    assert n == 7 * width
    tm = min(256, ctx_len)
    assert length % tm == 0 and ctx_len % tm == 0
    nx, nc = length // tm, ctx_len // tm
    add_pos = pos is not None

    in_specs = [pl.BlockSpec((1, tm, d), lambda bi, i: (bi, jnp.minimum(i, nx - 1), 0)),
                pl.BlockSpec((1, tm, d), lambda bi, i: (bi, jnp.maximum(i - nx, 0), 0))]
    args = [x, ctx]
    if add_pos:
        in_specs.append(pl.BlockSpec((tm, d), lambda bi, i: (jnp.minimum(i, nx - 1), 0)))
        args.append(pos)
    mod_row = lambda bi, i: (bi + b * (i // nx), 0, 0)
    in_specs += [
        pl.BlockSpec((1, 1, d), lambda bi, i: (layer, 0, 0)),
        pl.BlockSpec((1, 1, d), mod_row),
        pl.BlockSpec((1, 1, d), mod_row),
        pl.BlockSpec((1, 1, width), lambda bi, i: (layer, 0, 0)),
        pl.BlockSpec((1, 1, width), lambda bi, i: (depth + layer, 0, 0)),
        pl.BlockSpec((1, d, n), lambda bi, i: (layer, 0, 0), pipeline_mode=pl.Buffered(1)),
    ]
    lb_rows = lbs.reshape(2 * depth, 1, width)
    args += [norm_w.reshape(depth, 1, d), scale.reshape(2 * b, 1, d), shift.reshape(2 * b, 1, d),
             lb_rows, lb_rows, w_bf16]
    n_slabs = N_SLAB_GROUPS * width // LANES
    return pl.pallas_call(
        functools.partial(_inproj_kernel, add_pos, width, nx),
        grid=(b, nx + nc),
        in_specs=in_specs,
        out_specs=pl.BlockSpec((1, n_slabs, tm, LANES), lambda bi, i: (bi, 0, i, 0)),
        out_shape=jax.ShapeDtypeStruct((b, n_slabs, length + ctx_len, LANES), BF16),
        compiler_params=pltpu.CompilerParams(
            dimension_semantics=("parallel", "parallel"), vmem_limit_bytes=VMEM_LIMIT),
        name="inproj",
    )(*args)


def _chunk_cumsum(tri, g_ref, rows):
    out = _dot(tri, jnp.concatenate([g_ref[0, 0, rows, :], g_ref[0, 1, rows, :]], axis=1))
    return out[:, :HEAD_DIM] + out[:, HEAD_DIM:]


def _half_level(q, k, bc, w, forward):
    c = q.shape[0]
    qs, ks = [], []
    zeros = jnp.zeros((w, HEAD_DIM), F32)
    for j in range(c // (2 * w)):
        lo = slice(2 * w * j, 2 * w * j + w)
        hi = slice(2 * w * j + w, 2 * w * (j + 1))
        if forward:
            ref = bc[2 * w * j + w - 1:2 * w * j + w, :]
            qs.append(q[hi] * jnp.exp2(bc[hi] - ref))
            ks += [k[lo] * jnp.exp2(ref - bc[lo]), zeros]
        else:
            ref = bc[2 * w * j + w:2 * w * j + w + 1, :]
            qs.append(q[lo] * jnp.exp2(bc[lo] - ref))
            ks += [zeros, k[hi] * jnp.exp2(ref - bc[hi])]
    qt = jnp.concatenate(qs, axis=0).astype(BF16)
    kt = jnp.concatenate(ks, axis=0).astype(BF16)
    return _dot_nt(qt, kt)


def _base_level(q, k, bc):
    c = q.shape[0]
    blocks = c // BASE_BLOCK
    mid = BASE_BLOCK // 2
    bc3 = bc.reshape(blocks, BASE_BLOCK, HEAD_DIM)
    e = bc3 - bc3[:, mid - 1:mid, :]
    qt = (q.reshape(blocks, BASE_BLOCK, HEAD_DIM) * jnp.exp2(e)).reshape(c, HEAD_DIM).astype(BF16)
    kt = (k.reshape(blocks, BASE_BLOCK, HEAD_DIM) * jnp.exp2(-e)).reshape(c, HEAD_DIM).astype(BF16)
    return _dot_nt(qt, kt)


def _level_products(q, kf, bcf, kb, bcb):
    c = q.shape[0]
    prods = {(0, True): _base_level(q, kf, bcf), (0, False): _base_level(q, kb, bcb)}
    w = BASE_BLOCK
    while w < c:
        prods[(w, True)] = _half_level(q, kf, bcf, w, True)
        prods[(w, False)] = _half_level(q, kb, bcb, w, False)
        w *= 2
    return prods


def _assemble_scores(prods, code_ref):
    c = prods[(0, True)].shape[0]
    nb = c // BASE_BLOCK
    blocks = []
    for r in range(nb):
        rows = slice(r * BASE_BLOCK, (r + 1) * BASE_BLOCK)
        code = code_ref[rows, :]
        f, b = prods[(0, True)][rows], prods[(0, False)][rows]
        blocks.append(jnp.where(code == 0, f + b, jnp.where(code == 1, f, jnp.where(code == -1, b, 0.0))))
    w = BASE_BLOCK
    level = 2
    while w < c:
        for forward, sign in ((True, 1), (False, -1)):
            r_w = prods[(w, forward)]
            for j in range(c // (2 * w)):
                first = 2 * w * j + (w if forward else 0)
                for r8 in range(w // BASE_BLOCK):
                    blk = first // BASE_BLOCK + r8
                    src = slice(j * w + r8 * BASE_BLOCK, j * w + (r8 + 1) * BASE_BLOCK)
                    code = code_ref[blk * BASE_BLOCK:(blk + 1) * BASE_BLOCK, :]
                    blocks[blk] = jnp.where(code == sign * level, r_w[src], blocks[blk])
        w *= 2
        level += 1
    return jnp.concatenate(blocks, axis=0)


def _scan_constants():
    c = SCAN_CHUNK
    t_idx = lax.broadcasted_iota(jnp.int32, (c, c), 0)
    s_idx = lax.broadcasted_iota(jnp.int32, (c, c), 1)
    tri_lo = (s_idx <= t_idx).astype(BF16)
    tri_up = (s_idx >= t_idx).astype(BF16)
    x = t_idx ^ s_idx
    lvl = jnp.ones((c, c), jnp.int32)
    w = BASE_BLOCK
    while w < c:
        lvl = lvl + (x >= w).astype(jnp.int32)
        w *= 2
    code = jnp.where(t_idx > s_idx, lvl, jnp.where(t_idx < s_idx, -lvl, 0))
    return tri_lo, tri_up, code


def _scan_kernel(n_x, n_c, v_ref, q_ref, kf_ref, kb_ref, gf_ref, gb_ref, ga_ref, hn_ref,
                 tlo_ref, tup_ref, code_ref, ya_ref, o_s, qt_s, u_s, d_s, st_s):
    c = SCAN_CHUNK
    hd = HEAD_DIM
    hn = hn_ref[0]
    tri_lo = tlo_ref[...]
    tri_up = tup_ref[...]

    def rows_of(i):
        return slice(i * c, (i + 1) * c)

    def gate_stage(i):
        rows = rows_of(i)
        return dict(i=i, rows=rows,
                    kf=kf_ref[0, 0, rows, :].astype(F32), kb=kb_ref[0, 0, rows, :].astype(F32),
                    q=q_ref[0, 0, rows, :].astype(F32),
                    bcf=_chunk_cumsum(tri_lo, gf_ref, rows), bcb=_chunk_cumsum(tri_up, gb_ref, rows))

    def decay_stage(s):
        q, kf, kb, bcf, bcb, rows = s["q"], s["kf"], s["kb"], s["bcf"], s["bcb"], s["rows"]
        s["prods"] = _level_products(q, kf, bcf, kb, bcb)
        qt_s[rows, :hd] = (q * jnp.exp2(bcf)).astype(BF16)
        qt_s[rows, hd:] = (q * jnp.exp2(bcb)).astype(BF16)
        ftot = bcf[c - 1:c, :]
        btot = bcb[0:1, :]
        khat = jnp.concatenate([kf * jnp.exp2(ftot - bcf), kb * jnp.exp2(btot - bcb)], axis=1)
        u_s[s["i"]] = _dot_tn(v_ref[0, 0, rows, :], khat.astype(BF16))
        d_s[s["i"]] = jnp.broadcast_to(jnp.concatenate([jnp.exp2(ftot), jnp.exp2(btot)], axis=1),
                                       (BASE_BLOCK, 2 * hd))

    def mix_stage(s):
        p = _assemble_scores(s["prods"], code_ref)
        o_s[s["rows"], :] = _dot(p.astype(BF16), v_ref[0, 0, s["rows"], :])

    n = n_x + n_c
    staged = {}
    for k in range(-STAGE_SKEW, n + STAGE_SKEW):
        if 0 <= k < n:
            decay_stage(staged[k])
        if 0 <= k + STAGE_SKEW < n:
            staged[k + STAGE_SKEW] = gate_stage(k + STAGE_SKEW)
        if 0 <= k - STAGE_SKEW < n:
            mix_stage(staged.pop(k - STAGE_SKEW))

    ctx_chunks = list(range(n_x, n_x + n_c))
    x_chunks = list(range(n_x))
    state = jnp.zeros((hd, hd), F32)
    for i in ctx_chunks + x_chunks:
        st_s[i, :, :hd] = state.astype(BF16)
        state = state * d_s[i, 0:1, :hd] + u_s[i, :, :hd]
    state = jnp.zeros((hd, hd), F32)
    for i in ctx_chunks[::-1] + x_chunks[::-1]:
        st_s[i, :, hd:] = state.astype(BF16)
        state = state * d_s[i, 0:1, hd:] + u_s[i, :, hd:]

    inter = {}
    for k in range(-STAGE_SKEW, n):
        if k + STAGE_SKEW < n:
            i = k + STAGE_SKEW
            inter[i] = _dot_nt(qt_s[rows_of(i), :], st_s[i])
        if k >= 0:
            rows = rows_of(k)
            o = o_s[rows, :] + inter.pop(k)
            ms = jnp.mean(o * o, axis=-1, keepdims=True)
            y = o * lax.rsqrt(ms + EPS) * hn
            ya_ref[0, rows, :] = (y * ga_ref[0, 0, rows, :].astype(F32)).astype(BF16)


def _scan(layer, p, hn_w, n_ctx_rows, consts):
    b, _, total, _ = p.shape
    depth, hgrn_width = hn_w.shape
    heads = hgrn_width // HEAD_DIM
    c = SCAN_CHUNK
    n_chunks = total // c
    n_c = n_ctx_rows // c

    def slab(group):
        return pl.BlockSpec((1, 1, total, HEAD_DIM), lambda bi, h: (bi, h + group * heads, 0, 0))

    def slab_pair(group):
        return pl.BlockSpec((1, 2, total, HEAD_DIM), lambda bi, h: (bi, h + group * heads // 2, 0, 0))

    const = pl.BlockSpec((c, c), lambda bi, h: (0, 0))
    return pl.pallas_call(
        functools.partial(_scan_kernel, n_chunks - n_c, n_c),
        grid=(b, heads),
        in_specs=[slab(G_V), slab(G_Q), slab(G_KF), slab(G_KB), slab_pair(G_GF), slab_pair(G_GB),
                  slab(G_GA), pl.BlockSpec((1, 1, HEAD_DIM), lambda bi, h: (layer * heads + h, 0, 0)),
                  const, const, const],
        out_specs=pl.BlockSpec((1, total, HEAD_DIM), lambda bi, h: (bi, 0, h)),
        out_shape=jax.ShapeDtypeStruct((b, total, heads * HEAD_DIM), BF16),
        scratch_shapes=[
            pltpu.VMEM((total, HEAD_DIM), F32),
            pltpu.VMEM((total, 2 * HEAD_DIM), BF16),
            pltpu.VMEM((n_chunks, HEAD_DIM, 2 * HEAD_DIM), F32),
            pltpu.VMEM((n_chunks, BASE_BLOCK, 2 * HEAD_DIM), F32),
            pltpu.VMEM((n_chunks, HEAD_DIM, 2 * HEAD_DIM), BF16),
        ],
        compiler_params=pltpu.CompilerParams(
            dimension_semantics=("parallel", "parallel"), vmem_limit_bytes=VMEM_LIMIT),
        name="scan",
    )(p, p, p, p, p, p, p, hn_w.reshape(depth * heads, 1, HEAD_DIM), *consts)


def _fourier_out_kernel(length, add_pos, final_norm, *refs):
    refs = list(refs)
    u_ref, g_ref, ab_ref, bias_ref, cs_ref, ya_ref, w_ref, x_ref, gate_ref = refs[:9]
    rest = refs[9:]
    pos_ref = rest.pop(0) if add_pos else None
    fnw_ref = rest.pop(0) if final_norm else None
    o_ref, tt_s = rest
    cg = GROUP_DIM
    slabs = cg // LANES
    half = length // 2

    @pl.when(pl.program_id(1) == 0)
    def _():
        l_idx = lax.broadcasted_iota(jnp.int32, (half, half), 0)
        m_idx = lax.broadcasted_iota(jnp.int32, (half, half), 1)
        rev = ((l_idx + m_idx == half) & (l_idx >= 1)).astype(BF16)
        def upper(g):
            return jnp.concatenate([u_ref[0, slabs * g + s, half:, :] for s in range(slabs)], axis=1)

        mirrored = [_dot(rev, upper(g)) for g in range(FOURIER_GROUPS)]
        for g in range(FOURIER_GROUPS):
            cols = slice(g * cg, (g + 1) * cg)
            lower = jnp.concatenate([u_ref[0, slabs * g + s, :half, :] for s in range(slabs)], axis=1)
            lower = lower.astype(F32)
            tt_s[:half, cols] = _dot((lower + mirrored[g]).astype(BF16), ab_ref[0, g, :cg, :]).astype(BF16)
            tt_s[half:, cols] = _dot((lower - mirrored[g]).astype(BF16), ab_ref[0, g, cg:, :]).astype(BF16)
            tt_s[half:half + 1, cols] = _dot(upper(g)[:16], ab_ref[0, g, :cg, :])[0:1].astype(BF16)

    acc = _dot(cs_ref[...], tt_s[...])
    gate_f = jnp.concatenate([g_ref[0, j] for j in range(acc.shape[1] // LANES)], axis=1)
    y_f = ((acc + bias_ref[0]) * gate_f.astype(F32)).astype(BF16)
    wa = ya_ref.shape[2]
    out = _dot(ya_ref[0], w_ref[0, :wa, :]) + _dot(y_f, w_ref[0, wa:, :])
    x = x_ref[0]
    if add_pos:
        x = x + pos_ref[...]
    x = x + gate_ref[0] * out
    if final_norm:
        ms = jnp.mean(x * x, axis=-1, keepdims=True)
        x = x * lax.rsqrt(ms + EPS) * fnw_ref[...]
    o_ref[0] = x


def _fourier_out(layer, p, y_a, row0, ab, bias, cs, w_bf16, x, gate, pos, final_norm_w):
    b, length, d = x.shape
    depth, groups, _, cg = ab.shape
    width = groups * cg
    wa = y_a.shape[2]
    slabs = width // LANES
    tm = min(512, length)
    assert row0 % length == 0
    u_block = G_U * HEADS_PER_GROUP // slabs
    gate_block = G_GATE_F * HEADS_PER_GROUP // slabs
    add_pos = pos is not None
    final_norm = final_norm_w is not None
    in_specs = [
        pl.BlockSpec((1, slabs, length, LANES), lambda bi, i: (bi, u_block, row0 // length, 0)),
        pl.BlockSpec((1, slabs, tm, LANES), lambda bi, i: (bi, gate_block, row0 // tm + i, 0)),
        pl.BlockSpec((1, groups, 2 * cg, cg), lambda bi, i: (layer, 0, 0, 0)),
        pl.BlockSpec((1, 1, width), lambda bi, i: (layer, 0, 0)),
        pl.BlockSpec((tm, length), lambda bi, i: (i, 0)),
        pl.BlockSpec((1, tm, wa), lambda bi, i: (bi, row0 // tm + i, 0)),
        pl.BlockSpec((1, wa + width, d), lambda bi, i: (layer, 0, 0)),
        pl.BlockSpec((1, tm, d), lambda bi, i: (bi, i, 0)),
        pl.BlockSpec((1, 1, d), lambda bi, i: (bi, 0, 0)),
    ]
    args = [p, p, ab, bias.reshape(depth, 1, width), cs, y_a, w_bf16, x, gate.reshape(b, 1, d)]
    if add_pos:
        in_specs.append(pl.BlockSpec((tm, d), lambda bi, i: (i, 0)))
        args.append(pos)
    if final_norm:
        in_specs.append(pl.BlockSpec((1, d), lambda bi, i: (0, 0)))
        args.append(final_norm_w.reshape(1, d))
    return pl.pallas_call(
        functools.partial(_fourier_out_kernel, length, add_pos, final_norm),
        grid=(b, length // tm),
        in_specs=in_specs,
        out_specs=pl.BlockSpec((1, tm, d), lambda bi, i: (bi, i, 0)),
        out_shape=jax.ShapeDtypeStruct((b, length, d), F32),
        scratch_shapes=[pltpu.VMEM((length, width), BF16)],
        compiler_params=pltpu.CompilerParams(
            dimension_semantics=("parallel", "arbitrary"), vmem_limit_bytes=VMEM_LIMIT),
        name="fourier_out",
    )(*args)


def _seq_dft_matrix(length):
    half = length // 2
    row = lax.broadcasted_iota(jnp.int32, (length, length), 0)
    col = lax.broadcasted_iota(jnp.int32, (length, length), 1)
    freq = jnp.where(col <= half, col, col - half)
    shift = jnp.where(col <= half, length // 4, half)
    ang = ((row * freq + shift) % length).astype(F32) * (2.0 * math.pi / length)
    return (jnp.sin(ang) * (1.0 / math.sqrt(length * GROUP_DIM))).astype(BF16)


def kernel(x, c, ctx, c_ctx, norm_w, w_ada, b_ada, w_in, lower_bounds, hgrn_norm_w,
           w_fourier, b_fourier, w_out, final_norm_w):
    b, length, d = x.shape
    depth = w_in.shape[0]
    ctx_len = ctx.shape[1]
    assert hgrn_norm_w.shape[1] == HEADS_PER_GROUP * HEAD_DIM

    pos = _sincos_2d(length, d)
    lbs = _lower_bounds(lower_bounds)
    rows = ((b + 1 + 7) // 8) * 8
    c_rows = jnp.zeros((rows, d), F32).at[:b].set(c).at[b].set(c_ctx)
    mod = _ada_all(c_rows, w_ada, b_ada)
    ab = _fold_fourier_weights(w_fourier)
    cs_x = _seq_dft_matrix(length)
    cs_c = _seq_dft_matrix(ctx_len)
    w_in_b = w_in.astype(BF16)
    w_out_b = w_out.astype(BF16)
    consts = _scan_constants()

    def both_streams(l, k):
        part = mod[l, :, k * d:(k + 1) * d]
        return jnp.concatenate([part[:b], jnp.broadcast_to(part[b:b + 1], (b, d))], axis=0)

    for l in range(depth):
        last = l == depth - 1
        shift, scale, gate = (both_streams(l, k) for k in range(3))
        p = _inproj(l, x, ctx, pos if l == 0 else None, norm_w, scale, shift, lbs, w_in_b)
        y_a = _scan(l, p, hgrn_norm_w, ctx_len, consts)
        x_next = _fourier_out(l, p, y_a, 0, ab, b_fourier, cs_x, w_out_b, x, gate[:b],
                              pos if l == 0 else None, final_norm_w if last else None)
        if not last:
            ctx = _fourier_out(l, p, y_a, length, ab, b_fourier, cs_c, w_out_b, ctx, gate[b:], None, None)
        x = x_next
    return x
```

```python
import functools
import math

import jax
import jax.numpy as jnp
from jax import lax
from jax.experimental import pallas as pl
from jax.experimental.pallas import tpu as pltpu

F32 = jnp.float32
BF16 = jnp.bfloat16

LANES = 128
HEAD_DIM = 128
GROUP_DIM = 256
FOURIER_GROUPS = 4
GRID_W = 64
POS_BASE = 10000.0
EPS = 1e-6
F_MIN = 1e-6
LOG2_E = 1.4426950408889634
SCAN_CHUNK = 128
BASE_BLOCK = 8
STAGE_SKEW = 4
SCAN_HEADS = 2
VMEM_LIMIT = 56 * 1024 * 1024


def _sigmoid(z):
    return 1.0 / (1.0 + jnp.exp(-z))


def _silu(z):
    return z * _sigmoid(z)


def _dot(a, b):
    return jnp.dot(a, b, preferred_element_type=F32)


def _dot_nt(a, b):
    return lax.dot_general(a, b, (((1,), (1,)), ((), ())), preferred_element_type=F32)


def _dot_tn(a, b):
    return lax.dot_general(a, b, (((0,), (0,)), ((), ())), preferred_element_type=F32)


def _ada_kernel(c_ref, w_ref, b_ref, o_ref):
    sc = _silu(c_ref[...]).astype(BF16)
    o_ref[0] = _dot(sc, w_ref[0].astype(BF16)) + b_ref[0]


def _ada_all(c_rows, w_ada, b_ada):
    depth, d, n = w_ada.shape
    rows = c_rows.shape[0]
    tn = 1024
    return pl.pallas_call(
        _ada_kernel,
        grid=(depth, n // tn),
        in_specs=[
            pl.BlockSpec((rows, d), lambda l, j: (0, 0)),
            pl.BlockSpec((1, d, tn), lambda l, j: (l, 0, j)),
            pl.BlockSpec((1, 1, tn), lambda l, j: (l, 0, j)),
        ],
        out_specs=pl.BlockSpec((1, rows, tn), lambda l, j: (l, 0, j)),
        out_shape=jax.ShapeDtypeStruct((depth, rows, n), F32),
        name="ada_mod",
    )(c_rows, w_ada, b_ada.reshape(depth, 1, n))


def _lower_bound_kernel(depth, lb_ref, o_ref):
    for dr in range(2):
        rows = [lb_ref[dr * depth + l:dr * depth + l + 1, :] for l in range(depth)]
        mx = functools.reduce(jnp.maximum, rows)
        ex = [jnp.exp(r - mx) for r in rows]
        den = functools.reduce(lambda a, b: a + b, ex)
        p = [e / den for e in ex]
        run = p[0]
        for l in range(depth):
            if l > 0:
                run = run + p[l]
            o_ref[dr * depth + l:dr * depth + l + 1, :] = run - p[0]


def _lower_bounds(lower_bounds):
    two, depth, width = lower_bounds.shape
    out = pl.pallas_call(
        functools.partial(_lower_bound_kernel, depth),
        out_shape=jax.ShapeDtypeStruct((two * depth, width), F32),
        name="lower_bounds",
    )(lower_bounds.reshape(two * depth, width))
    return out.reshape(two, depth, width)


def _fold_kernel(cs_ref, w_ref, o_ref):
    o_ref[0, 0] = jnp.dot(cs_ref[...], w_ref[0, 0], preferred_element_type=F32,
                          precision=lax.Precision.HIGHEST).astype(BF16)


def _fold_fourier_weights(w_fourier):
    depth, groups, cg, _ = w_fourier.shape
    cs = jnp.concatenate(_dft_tables(cg), axis=0)
    return pl.pallas_call(
        _fold_kernel,
        grid=(depth, groups),
        in_specs=[
            pl.BlockSpec((2 * cg, cg), lambda l, g: (0, 0)),
            pl.BlockSpec((1, 1, cg, cg), lambda l, g: (l, g, 0, 0)),
        ],
        out_specs=pl.BlockSpec((1, 1, 2 * cg, cg), lambda l, g: (l, g, 0, 0)),
        out_shape=jax.ShapeDtypeStruct((depth, groups, 2 * cg, cg), BF16),
        name="fold_fourier",
    )(cs, w_fourier)


def _dft_tables(n):
    j = lax.broadcasted_iota(jnp.int32, (n, n), 0)
    k = lax.broadcasted_iota(jnp.int32, (n, n), 1)
    ang = ((j * k) % n).astype(F32) * (2.0 * math.pi / n)
    return jnp.cos(ang), jnp.sin(ang)


def _sincos_2d(length, dim):
    rows = length // GRID_W
    row = jnp.repeat(jnp.arange(rows, dtype=F32), GRID_W)
    col = jnp.tile(jnp.arange(GRID_W, dtype=F32), rows)
    quarter = dim // 4
    omega = jnp.power(POS_BASE, -jnp.arange(quarter, dtype=F32) / quarter)

    def axis_code(p):
        ang = p[:, None] * omega[None, :]
        return jnp.concatenate([jnp.sin(ang), jnp.cos(ang)], axis=-1)

    return jnp.concatenate([axis_code(row), axis_code(col)], axis=-1)


G_V, G_Q, G_KF, G_KB, G_GF, G_GB, G_GA, G_U, G_GATE_F = 0, 1, 2, 3, 4, 6, 8, 9, 10
HEADS_PER_GROUP = 8
N_SLAB_GROUPS = 11
INPROJ_SUBTILE = 256


def _inproj_kernel(add_pos, width, n_x_tiles, *refs):
    if add_pos:
        x_ref, ctx_ref, pos_ref, nw_ref, sc_ref, sh_ref, lbf_ref, lbb_ref, w_ref, o_ref = refs
    else:
        x_ref, ctx_ref, nw_ref, sc_ref, sh_ref, lbf_ref, lbb_ref, w_ref, o_ref = refs
    x = x_ref[0]
    if add_pos:
        x = x + pos_ref[...]
    x = jnp.where(pl.program_id(1) >= n_x_tiles, ctx_ref[0], x)
    ms = jnp.mean(x * x, axis=-1, keepdims=True)
    y = x * lax.rsqrt(ms + EPS) * nw_ref[0]
    h = (y * (1.0 + sc_ref[0]) + sh_ref[0]).astype(BF16)
    per = width // LANES
    sub = INPROJ_SUBTILE // LANES

    def project(j, t):
        lo = j * width + t * INPROJ_SUBTILE
        return _dot(h, w_ref[0, :, lo:lo + INPROJ_SUBTILE])

    def put(group, t, values):
        for s in range(sub):
            o_ref[0, group * per + t * sub + s] = values[:, s * LANES:(s + 1) * LANES].astype(BF16)

    def put_forget(k_group, g_group, t, z, lb_ref):
        one_minus_lb = 1.0 - lb_ref[0, :, t * INPROJ_SUBTILE:(t + 1) * INPROJ_SUBTILE]
        k = one_minus_lb * (1.0 / (1.0 + jnp.exp(z)))
        g = jnp.log(jnp.maximum(1.0 - k, F_MIN)) * LOG2_E
        hi = g.astype(BF16)
        lo = (g - hi.astype(F32)).astype(BF16)
        put(k_group, t, k)
        for s in range(sub):
            o_ref[0, g_group * per + 2 * (t * sub + s)] = hi[:, s * LANES:(s + 1) * LANES]
            o_ref[0, g_group * per + 2 * (t * sub + s) + 1] = lo[:, s * LANES:(s + 1) * LANES]

    for t in range(width // INPROJ_SUBTILE):
        put(G_V, t, project(0, t))
        put_forget(G_KF, G_GF, t, project(1, t), lbf_ref)
        put_forget(G_KB, G_GB, t, project(2, t), lbb_ref)
        put(G_Q, t, _silu(project(3, t)))
        put(G_GA, t, _silu(project(4, t)))
        put(G_U, t, project(5, t))
        put(G_GATE_F, t, _silu(project(6, t)))


def _inproj(layer, x, ctx, pos, norm_w, scale, shift, lbs, w_bf16):
    b, length, d = x.shape
    ctx_len = ctx.shape[1]
    depth, _, n = w_bf16.shape
    width = lbs.shape[2]
    assert n == 7 * width
    tm = min(256, ctx_len)
    assert length % tm == 0 and ctx_len % tm == 0
    nx, nc = length // tm, ctx_len // tm
    add_pos = pos is not None

    in_specs = [pl.BlockSpec((1, tm, d), lambda bi, i: (bi, jnp.minimum(i, nx - 1), 0)),
                pl.BlockSpec((1, tm, d), lambda bi, i: (bi, jnp.maximum(i - nx, 0), 0))]
    args = [x, ctx]
    if add_pos:
        in_specs.append(pl.BlockSpec((tm, d), lambda bi, i: (jnp.minimum(i, nx - 1), 0)))
        args.append(pos)
    mod_row = lambda bi, i: (bi + b * (i // nx), 0, 0)
    in_specs += [
        pl.BlockSpec((1, 1, d), lambda bi, i: (layer, 0, 0)),
        pl.BlockSpec((1, 1, d), mod_row),
        pl.BlockSpec((1, 1, d), mod_row),
        pl.BlockSpec((1, 1, width), lambda bi, i: (layer, 0, 0)),
        pl.BlockSpec((1, 1, width), lambda bi, i: (depth + layer, 0, 0)),
        pl.BlockSpec((1, d, n), lambda bi, i: (layer, 0, 0), pipeline_mode=pl.Buffered(1)),
    ]
    lb_rows = lbs.reshape(2 * depth, 1, width)
    args += [norm_w.reshape(depth, 1, d), scale.reshape(2 * b, 1, d), shift.reshape(2 * b, 1, d),
             lb_rows, lb_rows, w_bf16]
    n_slabs = N_SLAB_GROUPS * width // LANES
    return pl.pallas_call(
        functools.partial(_inproj_kernel, add_pos, width, nx),
        grid=(b, nx + nc),
        in_specs=in_specs,
        out_specs=pl.BlockSpec((1, n_slabs, tm, LANES), lambda bi, i: (bi, 0, i, 0)),
        out_shape=jax.ShapeDtypeStruct((b, n_slabs, length + ctx_len, LANES), BF16),
        compiler_params=pltpu.CompilerParams(
            dimension_semantics=("parallel", "parallel"), vmem_limit_bytes=VMEM_LIMIT),
        name="inproj",
    )(*args)


def _chunk_cumsum(tri, g_ref, head, rows):
    out = _dot(tri, jnp.concatenate([g_ref[0, 2 * head, rows, :], g_ref[0, 2 * head + 1, rows, :]], axis=1))
    return out[:, :HEAD_DIM] + out[:, HEAD_DIM:]


def _half_level(q, k, bc, w, forward):
    c = q.shape[0]
    qs, ks = [], []
    zeros = jnp.zeros((w, HEAD_DIM), F32)
    for j in range(c // (2 * w)):
        lo = slice(2 * w * j, 2 * w * j + w)
        hi = slice(2 * w * j + w, 2 * w * (j + 1))
        if forward:
            ref = bc[2 * w * j + w - 1:2 * w * j + w, :]
            qs.append(q[hi] * jnp.exp2(bc[hi] - ref))
            ks += [k[lo] * jnp.exp2(ref - bc[lo]), zeros]
        else:
            ref = bc[2 * w * j + w:2 * w * j + w + 1, :]
            qs.append(q[lo] * jnp.exp2(bc[lo] - ref))
            ks += [zeros, k[hi] * jnp.exp2(ref - bc[hi])]
    qt = jnp.concatenate(qs, axis=0).astype(BF16)
    kt = jnp.concatenate(ks, axis=0).astype(BF16)
    return _dot_nt(qt, kt)


def _base_level(q, k, bc):
    c = q.shape[0]
    blocks = c // BASE_BLOCK
    mid = BASE_BLOCK // 2
    bc3 = bc.reshape(blocks, BASE_BLOCK, HEAD_DIM)
    e = bc3 - bc3[:, mid - 1:mid, :]
    qt = (q.reshape(blocks, BASE_BLOCK, HEAD_DIM) * jnp.exp2(e)).reshape(c, HEAD_DIM).astype(BF16)
    kt = (k.reshape(blocks, BASE_BLOCK, HEAD_DIM) * jnp.exp2(-e)).reshape(c, HEAD_DIM).astype(BF16)
    return _dot_nt(qt, kt)


def _level_products(q, kf, bcf, kb, bcb):
    c = q.shape[0]
    prods = {(0, True): _base_level(q, kf, bcf), (0, False): _base_level(q, kb, bcb)}
    w = BASE_BLOCK
    while w < c:
        prods[(w, True)] = _half_level(q, kf, bcf, w, True)
        prods[(w, False)] = _half_level(q, kb, bcb, w, False)
        w *= 2
    return prods


def _assemble_scores(prods, code_ref):
    c = prods[(0, True)].shape[0]
    nb = c // BASE_BLOCK
    blocks = []
    for r in range(nb):
        rows = slice(r * BASE_BLOCK, (r + 1) * BASE_BLOCK)
        code = code_ref[rows, :]
        f, b = prods[(0, True)][rows], prods[(0, False)][rows]
        blocks.append(jnp.where(code == 0, f + b, jnp.where(code == 1, f, jnp.where(code == -1, b, 0.0))))
    w = BASE_BLOCK
    level = 2
    while w < c:
        for forward, sign in ((True, 1), (False, -1)):
            r_w = prods[(w, forward)]
            for j in range(c // (2 * w)):
                first = 2 * w * j + (w if forward else 0)
                for r8 in range(w // BASE_BLOCK):
                    blk = first // BASE_BLOCK + r8
                    src = slice(j * w + r8 * BASE_BLOCK, j * w + (r8 + 1) * BASE_BLOCK)
                    code = code_ref[blk * BASE_BLOCK:(blk + 1) * BASE_BLOCK, :]
                    blocks[blk] = jnp.where(code == sign * level, r_w[src], blocks[blk])
        w *= 2
        level += 1
    return jnp.concatenate(blocks, axis=0)


def _scan_constants():
    c = SCAN_CHUNK
    t_idx = lax.broadcasted_iota(jnp.int32, (c, c), 0)
    s_idx = lax.broadcasted_iota(jnp.int32, (c, c), 1)
    tri_lo = (s_idx <= t_idx).astype(BF16)
    tri_up = (s_idx >= t_idx).astype(BF16)
    x = t_idx ^ s_idx
    lvl = jnp.ones((c, c), jnp.int32)
    w = BASE_BLOCK
    while w < c:
        lvl = lvl + (x >= w).astype(jnp.int32)
        w *= 2
    code = jnp.where(t_idx > s_idx, lvl, jnp.where(t_idx < s_idx, -lvl, 0))
    return tri_lo, tri_up, code


def _scan_kernel(n_x, n_c, v_ref, q_ref, kf_ref, kb_ref, gf_ref, gb_ref, ga_ref, hn_ref,
                 tlo_ref, tup_ref, code_ref, ya_ref, o_s, qt_s, u_s, d_s, st_s):
    c = SCAN_CHUNK
    hd = HEAD_DIM
    n = n_x + n_c
    tri_lo = tlo_ref[...]
    tri_up = tup_ref[...]

    def rows_of(i):
        return slice(i * c, (i + 1) * c)

    def gate_stage(hh, i):
        rows = rows_of(i)
        return dict(i=i, rows=rows,
                    kf=kf_ref[0, hh, rows, :].astype(F32), kb=kb_ref[0, hh, rows, :].astype(F32),
                    q=q_ref[0, hh, rows, :].astype(F32),
                    bcf=_chunk_cumsum(tri_lo, gf_ref, hh, rows), bcb=_chunk_cumsum(tri_up, gb_ref, hh, rows))

    def decay_stage(hh, s):
        q, kf, kb, bcf, bcb, rows = s["q"], s["kf"], s["kb"], s["bcf"], s["bcb"], s["rows"]
        s["prods"] = _level_products(q, kf, bcf, kb, bcb)
        qt_s[hh, rows, :hd] = (q * jnp.exp2(bcf)).astype(BF16)
        qt_s[hh, rows, hd:] = (q * jnp.exp2(bcb)).astype(BF16)
        ftot = bcf[c - 1:c, :]
        btot = bcb[0:1, :]
        khat = jnp.concatenate([kf * jnp.exp2(ftot - bcf), kb * jnp.exp2(btot - bcb)], axis=1)
        u_s[hh, s["i"]] = _dot_tn(v_ref[0, hh, rows, :], khat.astype(BF16))
        d_s[hh, s["i"]] = jnp.broadcast_to(jnp.concatenate([jnp.exp2(ftot), jnp.exp2(btot)], axis=1),
                                           (BASE_BLOCK, 2 * hd))

    def mix_stage(hh, s):
        p = _assemble_scores(s["prods"], code_ref)
        o_s[hh, s["rows"], :] = _dot(p.astype(BF16), v_ref[0, hh, s["rows"], :])

    for hh in range(SCAN_HEADS):
        staged = {}
        for k in range(-STAGE_SKEW, n + STAGE_SKEW):
            if 0 <= k < n:
                decay_stage(hh, staged[k])
            if 0 <= k + STAGE_SKEW < n:
                staged[k + STAGE_SKEW] = gate_stage(hh, k + STAGE_SKEW)
            if 0 <= k - STAGE_SKEW < n:
                mix_stage(hh, staged.pop(k - STAGE_SKEW))

    ctx_chunks = list(range(n_x, n))
    x_chunks = list(range(n_x))
    for hh in range(SCAN_HEADS):
        state = jnp.zeros((hd, hd), F32)
        for i in ctx_chunks + x_chunks:
            st_s[hh, i, :, :hd] = state.astype(BF16)
            state = state * d_s[hh, i, 0:1, :hd] + u_s[hh, i, :, :hd]
        state = jnp.zeros((hd, hd), F32)
        for i in ctx_chunks[::-1] + x_chunks[::-1]:
            st_s[hh, i, :, hd:] = state.astype(BF16)
            state = state * d_s[hh, i, 0:1, hd:] + u_s[hh, i, :, hd:]

    for hh in range(SCAN_HEADS):
        hn = hn_ref[hh]
        inter = {}
        for k in range(-STAGE_SKEW, n):
            if k + STAGE_SKEW < n:
                i = k + STAGE_SKEW
                inter[i] = _dot_nt(qt_s[hh, rows_of(i), :], st_s[hh, i])
            if k >= 0:
                rows = rows_of(k)
                o = o_s[hh, rows, :] + inter.pop(k)
                ms = jnp.mean(o * o, axis=-1, keepdims=True)
                y = o * lax.rsqrt(ms + EPS) * hn
                ya_ref[0, rows, hh * hd:(hh + 1) * hd] = (y * ga_ref[0, hh, rows, :].astype(F32)).astype(BF16)


def _scan(layer, p, hn_w, n_ctx_rows, consts):
    b, _, total, _ = p.shape
    depth, hgrn_width = hn_w.shape
    heads = hgrn_width // HEAD_DIM
    hs = SCAN_HEADS
    assert heads % hs == 0
    c = SCAN_CHUNK
    n_chunks = total // c
    n_c = n_ctx_rows // c

    def slab(group):
        return pl.BlockSpec((1, hs, total, HEAD_DIM), lambda bi, h: (bi, h + group * heads // hs, 0, 0))

    def slab_pair(group):
        return pl.BlockSpec((1, 2 * hs, total, HEAD_DIM), lambda bi, h: (bi, h + group * heads // (2 * hs), 0, 0))

    const = pl.BlockSpec((c, c), lambda bi, h: (0, 0))
    return pl.pallas_call(
        functools.partial(_scan_kernel, n_chunks - n_c, n_c),
        grid=(b, heads // hs),
        in_specs=[slab(G_V), slab(G_Q), slab(G_KF), slab(G_KB), slab_pair(G_GF), slab_pair(G_GB),
                  slab(G_GA), pl.BlockSpec((hs, 1, HEAD_DIM), lambda bi, h: (layer * heads // hs + h, 0, 0)),
                  const, const, const],
        out_specs=pl.BlockSpec((1, total, hs * HEAD_DIM), lambda bi, h: (bi, 0, h)),
        out_shape=jax.ShapeDtypeStruct((b, total, heads * HEAD_DIM), BF16),
        scratch_shapes=[
            pltpu.VMEM((hs, total, HEAD_DIM), F32),
            pltpu.VMEM((hs, total, 2 * HEAD_DIM), BF16),
            pltpu.VMEM((hs, n_chunks, HEAD_DIM, 2 * HEAD_DIM), F32),
            pltpu.VMEM((hs, n_chunks, BASE_BLOCK, 2 * HEAD_DIM), F32),
            pltpu.VMEM((hs, n_chunks, HEAD_DIM, 2 * HEAD_DIM), BF16),
        ],
        compiler_params=pltpu.CompilerParams(
            dimension_semantics=("parallel", "parallel"), vmem_limit_bytes=VMEM_LIMIT),
        name="scan",
    )(p, p, p, p, p, p, p, hn_w.reshape(depth * heads, 1, HEAD_DIM), *consts)


def _fourier_out_kernel(length, add_pos, final_norm, *refs):
    refs = list(refs)
    u_ref, g_ref, ab_ref, bias_ref, cs_ref, ya_ref, w_ref, x_ref, gate_ref = refs[:9]
    rest = refs[9:]
    pos_ref = rest.pop(0) if add_pos else None
    fnw_ref = rest.pop(0) if final_norm else None
    o_ref, tt_s = rest
    cg = GROUP_DIM
    slabs = cg // LANES
    half = length // 2

    @pl.when(pl.program_id(1) == 0)
    def _():
        l_idx = lax.broadcasted_iota(jnp.int32, (half, half), 0)
        m_idx = lax.broadcasted_iota(jnp.int32, (half, half), 1)
        rev = ((l_idx + m_idx == half) & (l_idx >= 1)).astype(BF16)
        def upper(g):
            return jnp.concatenate([u_ref[0, slabs * g + s, half:, :] for s in range(slabs)], axis=1)

        mirrored = [_dot(rev, upper(g)) for g in range(FOURIER_GROUPS)]
        for g in range(FOURIER_GROUPS):
            cols = slice(g * cg, (g + 1) * cg)
            lower = jnp.concatenate([u_ref[0, slabs * g + s, :half, :] for s in range(slabs)], axis=1)
            lower = lower.astype(F32)
            tt_s[:half, cols] = _dot((lower + mirrored[g]).astype(BF16), ab_ref[0, g, :cg, :]).astype(BF16)
            tt_s[half:, cols] = _dot((lower - mirrored[g]).astype(BF16), ab_ref[0, g, cg:, :]).astype(BF16)
            tt_s[half:half + 1, cols] = _dot(upper(g)[:16], ab_ref[0, g, :cg, :])[0:1].astype(BF16)

    acc = _dot(cs_ref[...], tt_s[...])
    gate_f = jnp.concatenate([g_ref[0, j] for j in range(acc.shape[1] // LANES)], axis=1)
    y_f = ((acc + bias_ref[0]) * gate_f.astype(F32)).astype(BF16)
    wa = ya_ref.shape[2]
    out = _dot(ya_ref[0], w_ref[0, :wa, :]) + _dot(y_f, w_ref[0, wa:, :])
    x = x_ref[0]
    if add_pos:
        x = x + pos_ref[...]
    x = x + gate_ref[0] * out
    if final_norm:
        ms = jnp.mean(x * x, axis=-1, keepdims=True)
        x = x * lax.rsqrt(ms + EPS) * fnw_ref[...]
    o_ref[0] = x


def _fourier_out(layer, p, y_a, row0, ab, bias, cs, w_bf16, x, gate, pos, final_norm_w):
    b, length, d = x.shape
    depth, groups, _, cg = ab.shape
    width = groups * cg
    wa = y_a.shape[2]
    slabs = width // LANES
    tm = min(512, length)
    assert row0 % length == 0
    u_block = G_U * HEADS_PER_GROUP // slabs
    gate_block = G_GATE_F * HEADS_PER_GROUP // slabs
    add_pos = pos is not None
    final_norm = final_norm_w is not None
    in_specs = [
        pl.BlockSpec((1, slabs, length, LANES), lambda bi, i: (bi, u_block, row0 // length, 0)),
        pl.BlockSpec((1, slabs, tm, LANES), lambda bi, i: (bi, gate_block, row0 // tm + i, 0)),
        pl.BlockSpec((1, groups, 2 * cg, cg), lambda bi, i: (layer, 0, 0, 0)),
        pl.BlockSpec((1, 1, width), lambda bi, i: (layer, 0, 0)),
        pl.BlockSpec((tm, length), lambda bi, i: (i, 0)),
        pl.BlockSpec((1, tm, wa), lambda bi, i: (bi, row0 // tm + i, 0)),
        pl.BlockSpec((1, wa + width, d), lambda bi, i: (layer, 0, 0)),
        pl.BlockSpec((1, tm, d), lambda bi, i: (bi, i, 0)),
        pl.BlockSpec((1, 1, d), lambda bi, i: (bi, 0, 0)),
    ]
    args = [p, p, ab, bias.reshape(depth, 1, width), cs, y_a, w_bf16, x, gate.reshape(b, 1, d)]
    if add_pos:
        in_specs.append(pl.BlockSpec((tm, d), lambda bi, i: (i, 0)))
        args.append(pos)
    if final_norm:
        in_specs.append(pl.BlockSpec((1, d), lambda bi, i: (0, 0)))
        args.append(final_norm_w.reshape(1, d))
    return pl.pallas_call(
        functools.partial(_fourier_out_kernel, length, add_pos, final_norm),
        grid=(b, length // tm),
        in_specs=in_specs,
        out_specs=pl.BlockSpec((1, tm, d), lambda bi, i: (bi, i, 0)),
        out_shape=jax.ShapeDtypeStruct((b, length, d), F32),
        scratch_shapes=[pltpu.VMEM((length, width), BF16)],
        compiler_params=pltpu.CompilerParams(
            dimension_semantics=("parallel", "arbitrary"), vmem_limit_bytes=VMEM_LIMIT),
        name="fourier_out",
    )(*args)


def _seq_dft_matrix(length):
    half = length // 2
    col = jnp.arange(length, dtype=jnp.int32)[None, :]
    freq = jnp.where(col <= half, col, col - half)
    shift = jnp.where(col <= half, length // 4, half)
    a = jnp.arange(length // GRID_W, dtype=jnp.int32)[:, None]
    r = jnp.arange(GRID_W, dtype=jnp.int32)[:, None]
    step = 2.0 * math.pi / length
    alpha = ((GRID_W * a * freq) % length).astype(F32) * step
    beta = ((r * freq + shift) % length).astype(F32) * step
    table = (jnp.sin(alpha)[:, None, :] * jnp.cos(beta)[None, :, :]
             + jnp.cos(alpha)[:, None, :] * jnp.sin(beta)[None, :, :])
    return (table.reshape(length, length) * (1.0 / math.sqrt(length * GROUP_DIM))).astype(BF16)


def kernel(x, c, ctx, c_ctx, norm_w, w_ada, b_ada, w_in, lower_bounds, hgrn_norm_w,
           w_fourier, b_fourier, w_out, final_norm_w):
    b, length, d = x.shape
    depth = w_in.shape[0]
    ctx_len = ctx.shape[1]
    assert hgrn_norm_w.shape[1] == HEADS_PER_GROUP * HEAD_DIM

    pos = _sincos_2d(length, d)
    lbs = _lower_bounds(lower_bounds)
    rows = ((b + 1 + 7) // 8) * 8
    c_rows = jnp.zeros((rows, d), F32).at[:b].set(c).at[b].set(c_ctx)
    mod = _ada_all(c_rows, w_ada, b_ada)
    ab = _fold_fourier_weights(w_fourier)
    cs_x = _seq_dft_matrix(length)
    cs_c = _seq_dft_matrix(ctx_len)
    w_in_b = w_in.astype(BF16)
    w_out_b = w_out.astype(BF16)
    consts = _scan_constants()

    def both_streams(l, k):
        part = mod[l, :, k * d:(k + 1) * d]
        return jnp.concatenate([part[:b], jnp.broadcast_to(part[b:b + 1], (b, d))], axis=0)

    for l in range(depth):
        last = l == depth - 1
        shift, scale, gate = (both_streams(l, k) for k in range(3))
        p = _inproj(l, x, ctx, pos if l == 0 else None, norm_w, scale, shift, lbs, w_in_b)
        y_a = _scan(l, p, hgrn_norm_w, ctx_len, consts)
        x_next = _fourier_out(l, p, y_a, 0, ab, b_fourier, cs_x, w_out_b, x, gate[:b],
                              pos if l == 0 else None, final_norm_w if last else None)
        if not last:
            ctx = _fourier_out(l, p, y_a, length, ab, b_fourier, cs_c, w_out_b, ctx, gate[b:], None, None)
        x = x_next
    return x
```

```python
import functools
import math

import jax
import jax.numpy as jnp
from jax import lax
from jax.experimental import pallas as pl
from jax.experimental.pallas import tpu as pltpu

F32 = jnp.float32
BF16 = jnp.bfloat16

LANES = 128
HEAD_DIM = 128
GROUP_DIM = 256
FOURIER_GROUPS = 4
GRID_W = 64
POS_BASE = 10000.0
EPS = 1e-6
F_MIN = 1e-6
LOG2_E = 1.4426950408889634
SCAN_CHUNK = 128
BASE_BLOCK = 8
STAGE_SKEW = 4
SCAN_HEADS = 2
VMEM_LIMIT = 56 * 1024 * 1024


def _sigmoid(z):
    return 1.0 / (1.0 + jnp.exp(-z))


def _silu(z):
    return z * _sigmoid(z)


def _dot(a, b):
    return jnp.dot(a, b, preferred_element_type=F32)


def _dot_nt(a, b):
    return lax.dot_general(a, b, (((1,), (1,)), ((), ())), preferred_element_type=F32)


def _dot_tn(a, b):
    return lax.dot_general(a, b, (((0,), (0,)), ((), ())), preferred_element_type=F32)


def _ada_kernel(c_ref, w_ref, b_ref, o_ref):
    sc = _silu(c_ref[...]).astype(BF16)
    o_ref[0] = _dot(sc, w_ref[0].astype(BF16)) + b_ref[0]


def _ada_all(c_rows, w_ada, b_ada):
    depth, d, n = w_ada.shape
    rows = c_rows.shape[0]
    tn = 1024
    return pl.pallas_call(
        _ada_kernel,
        grid=(depth, n // tn),
        in_specs=[
            pl.BlockSpec((rows, d), lambda l, j: (0, 0)),
            pl.BlockSpec((1, d, tn), lambda l, j: (l, 0, j)),
            pl.BlockSpec((1, 1, tn), lambda l, j: (l, 0, j)),
        ],
        out_specs=pl.BlockSpec((1, rows, tn), lambda l, j: (l, 0, j)),
        out_shape=jax.ShapeDtypeStruct((depth, rows, n), F32),
        name="ada_mod",
    )(c_rows, w_ada, b_ada.reshape(depth, 1, n))


def _lower_bound_kernel(depth, lb_ref, o_ref):
    for dr in range(2):
        rows = [lb_ref[dr * depth + l:dr * depth + l + 1, :] for l in range(depth)]
        mx = functools.reduce(jnp.maximum, rows)
        ex = [jnp.exp(r - mx) for r in rows]
        den = functools.reduce(lambda a, b: a + b, ex)
        p = [e / den for e in ex]
        run = p[0]
        for l in range(depth):
            if l > 0:
                run = run + p[l]
            o_ref[dr * depth + l:dr * depth + l + 1, :] = run - p[0]


def _lower_bounds(lower_bounds):
    two, depth, width = lower_bounds.shape
    out = pl.pallas_call(
        functools.partial(_lower_bound_kernel, depth),
        out_shape=jax.ShapeDtypeStruct((two * depth, width), F32),
        name="lower_bounds",
    )(lower_bounds.reshape(two * depth, width))
    return out.reshape(two, depth, width)


def _fold_kernel(cs_ref, w_ref, o_ref):
    o_ref[0, 0] = jnp.dot(cs_ref[...], w_ref[0, 0], preferred_element_type=F32,
                          precision=lax.Precision.HIGHEST).astype(BF16)


def _fold_fourier_weights(w_fourier):
    depth, groups, cg, _ = w_fourier.shape
    cs = jnp.concatenate(_dft_tables(cg), axis=0)
    return pl.pallas_call(
        _fold_kernel,
        grid=(depth, groups),
        in_specs=[
            pl.BlockSpec((2 * cg, cg), lambda l, g: (0, 0)),
            pl.BlockSpec((1, 1, cg, cg), lambda l, g: (l, g, 0, 0)),
        ],
        out_specs=pl.BlockSpec((1, 1, 2 * cg, cg), lambda l, g: (l, g, 0, 0)),
        out_shape=jax.ShapeDtypeStruct((depth, groups, 2 * cg, cg), BF16),
        name="fold_fourier",
    )(cs, w_fourier)


def _dft_tables(n):
    j = lax.broadcasted_iota(jnp.int32, (n, n), 0)
    k = lax.broadcasted_iota(jnp.int32, (n, n), 1)
    ang = ((j * k) % n).astype(F32) * (2.0 * math.pi / n)
    return jnp.cos(ang), jnp.sin(ang)


def _sincos_2d(length, dim):
    rows = length // GRID_W
    row = jnp.repeat(jnp.arange(rows, dtype=F32), GRID_W)
    col = jnp.tile(jnp.arange(GRID_W, dtype=F32), rows)
    quarter = dim // 4
    omega = jnp.power(POS_BASE, -jnp.arange(quarter, dtype=F32) / quarter)

    def axis_code(p):
        ang = p[:, None] * omega[None, :]
        return jnp.concatenate([jnp.sin(ang), jnp.cos(ang)], axis=-1)

    return jnp.concatenate([axis_code(row), axis_code(col)], axis=-1)


G_V, G_Q, G_KF, G_KB, G_GF, G_GB, G_GA, G_U, G_GATE_F = 0, 1, 2, 3, 4, 6, 8, 9, 10
HEADS_PER_GROUP = 8
N_SLAB_GROUPS = 11
INPROJ_SUBTILE = 256


def _inproj_kernel(add_pos, width, n_x_tiles, *refs):
    if add_pos:
        x_ref, ctx_ref, pos_ref, nw_ref, sc_ref, sh_ref, lbf_ref, lbb_ref, w_ref, o_ref = refs
    else:
        x_ref, ctx_ref, nw_ref, sc_ref, sh_ref, lbf_ref, lbb_ref, w_ref, o_ref = refs
    x = x_ref[0]
    if add_pos:
        x = x + pos_ref[...]
    x = jnp.where(pl.program_id(1) >= n_x_tiles, ctx_ref[0], x)
    ms = jnp.mean(x * x, axis=-1, keepdims=True)
    y = x * lax.rsqrt(ms + EPS) * nw_ref[0]
    h = (y * (1.0 + sc_ref[0]) + sh_ref[0]).astype(BF16)
    per = width // LANES
    sub = INPROJ_SUBTILE // LANES

    def project(j, t):
        lo = j * width + t * INPROJ_SUBTILE
        return _dot(h, w_ref[0, :, lo:lo + INPROJ_SUBTILE])

    def put(group, t, values):
        for s in range(sub):
            o_ref[0, group * per + t * sub + s] = values[:, s * LANES:(s + 1) * LANES].astype(BF16)

    def put_forget(k_group, g_group, t, z, lb_ref):
        one_minus_lb = 1.0 - lb_ref[0, :, t * INPROJ_SUBTILE:(t + 1) * INPROJ_SUBTILE]
        k = one_minus_lb * (1.0 / (1.0 + jnp.exp(z)))
        g = jnp.log(jnp.maximum(1.0 - k, F_MIN)) * LOG2_E
        hi = g.astype(BF16)
        lo = (g - hi.astype(F32)).astype(BF16)
        put(k_group, t, k)
        for s in range(sub):
            o_ref[0, g_group * per + 2 * (t * sub + s)] = hi[:, s * LANES:(s + 1) * LANES]
            o_ref[0, g_group * per + 2 * (t * sub + s) + 1] = lo[:, s * LANES:(s + 1) * LANES]

    for t in range(width // INPROJ_SUBTILE):
        put(G_V, t, project(0, t))
        put_forget(G_KF, G_GF, t, project(1, t), lbf_ref)
        put_forget(G_KB, G_GB, t, project(2, t), lbb_ref)
        put(G_Q, t, _silu(project(3, t)))
        put(G_GA, t, _silu(project(4, t)))
        put(G_U, t, project(5, t))
        put(G_GATE_F, t, _silu(project(6, t)))


def _inproj(layer, x, ctx, pos, norm_w, scale, shift, lbs, w_bf16):
    b, length, d = x.shape
    ctx_len = ctx.shape[1]
    depth, _, n = w_bf16.shape
    width = lbs.shape[2]
    assert n == 7 * width
    tm = min(256, ctx_len)
    assert length % tm == 0 and ctx_len % tm == 0
    nx, nc = length // tm, ctx_len // tm
    add_pos = pos is not None

    in_specs = [pl.BlockSpec((1, tm, d), lambda bi, i: (bi, jnp.minimum(i, nx - 1), 0)),
                pl.BlockSpec((1, tm, d), lambda bi, i: (bi, jnp.maximum(i - nx, 0), 0))]
    args = [x, ctx]
    if add_pos:
        in_specs.append(pl.BlockSpec((tm, d), lambda bi, i: (jnp.minimum(i, nx - 1), 0)))
        args.append(pos)
    mod_row = lambda bi, i: (bi + b * (i // nx), 0, 0)
    in_specs += [
        pl.BlockSpec((1, 1, d), lambda bi, i: (layer, 0, 0)),
        pl.BlockSpec((1, 1, d), mod_row),
        pl.BlockSpec((1, 1, d), mod_row),
        pl.BlockSpec((1, 1, width), lambda bi, i: (layer, 0, 0)),
        pl.BlockSpec((1, 1, width), lambda bi, i: (depth + layer, 0, 0)),
        pl.BlockSpec((1, d, n), lambda bi, i: (layer, 0, 0), pipeline_mode=pl.Buffered(1)),
    ]
    lb_rows = lbs.reshape(2 * depth, 1, width)
    args += [norm_w.reshape(depth, 1, d), scale.reshape(2 * b, 1, d), shift.reshape(2 * b, 1, d),
             lb_rows, lb_rows, w_bf16]
    n_slabs = N_SLAB_GROUPS * width // LANES
    return pl.pallas_call(
        functools.partial(_inproj_kernel, add_pos, width, nx),
        grid=(b, nx + nc),
        in_specs=in_specs,
        out_specs=pl.BlockSpec((1, n_slabs, tm, LANES), lambda bi, i: (bi, 0, i, 0)),
        out_shape=jax.ShapeDtypeStruct((b, n_slabs, length + ctx_len, LANES), BF16),
        compiler_params=pltpu.CompilerParams(
            dimension_semantics=("parallel", "parallel"), vmem_limit_bytes=VMEM_LIMIT),
        name="inproj",
    )(*args)


def _chunk_cumsum(tri, g_ref, head, rows):
    out = _dot(tri, jnp.concatenate([g_ref[0, 2 * head, rows, :], g_ref[0, 2 * head + 1, rows, :]], axis=1))
    return out[:, :HEAD_DIM] + out[:, HEAD_DIM:]


def _pair_level(q, kf, bcf, kb, bcb, w):
    c = q.shape[0]
    qs, ks = [], []
    for j in range(c // (2 * w)):
        lo = slice(2 * w * j, 2 * w * j + w)
        hi = slice(2 * w * j + w, 2 * w * (j + 1))
        ref_f = bcf[2 * w * j + w - 1:2 * w * j + w, :]
        ref_b = bcb[2 * w * j + w:2 * w * j + w + 1, :]
        qs += [q[lo] * jnp.exp2(bcb[lo] - ref_b), q[hi] * jnp.exp2(bcf[hi] - ref_f)]
        ks += [kf[lo] * jnp.exp2(ref_f - bcf[lo]), kb[hi] * jnp.exp2(ref_b - bcb[hi])]
    qt = jnp.concatenate(qs, axis=0).astype(BF16)
    kt = jnp.concatenate(ks, axis=0).astype(BF16)
    return _dot_nt(qt, kt)


def _base_level(q, k, bc):
    c = q.shape[0]
    blocks = c // BASE_BLOCK
    mid = BASE_BLOCK // 2
    bc3 = bc.reshape(blocks, BASE_BLOCK, HEAD_DIM)
    e = bc3 - bc3[:, mid - 1:mid, :]
    qt = (q.reshape(blocks, BASE_BLOCK, HEAD_DIM) * jnp.exp2(e)).reshape(c, HEAD_DIM).astype(BF16)
    kt = (k.reshape(blocks, BASE_BLOCK, HEAD_DIM) * jnp.exp2(-e)).reshape(c, HEAD_DIM).astype(BF16)
    return _dot_nt(qt, kt)


def _level_products(q, kf, bcf, kb, bcb):
    c = q.shape[0]
    prods = {(0, True): _base_level(q, kf, bcf), (0, False): _base_level(q, kb, bcb)}
    w = BASE_BLOCK
    while w < c:
        prods[w] = _pair_level(q, kf, bcf, kb, bcb, w)
        w *= 2
    return prods


def _assemble_scores(prods, code_ref):
    c = prods[(0, True)].shape[0]
    nb = c // BASE_BLOCK
    blocks = []
    for r in range(nb):
        rows = slice(r * BASE_BLOCK, (r + 1) * BASE_BLOCK)
        code = code_ref[rows, :]
        f, b = prods[(0, True)][rows], prods[(0, False)][rows]
        blocks.append(jnp.where(code == 0, f + b, jnp.where(code == 1, f, jnp.where(code == -1, b, 0.0))))
    w = BASE_BLOCK
    level = 2
    while w < c:
        r_w = prods[w]
        for blk in range(nb):
            rows = slice(blk * BASE_BLOCK, (blk + 1) * BASE_BLOCK)
            later_half = (blk * BASE_BLOCK // w) % 2 == 1
            blocks[blk] = jnp.where(code_ref[rows, :] == (level if later_half else -level), r_w[rows], blocks[blk])
        w *= 2
        level += 1
    return jnp.concatenate(blocks, axis=0)


def _scan_constants():
    c = SCAN_CHUNK
    t_idx = lax.broadcasted_iota(jnp.int32, (c, c), 0)
    s_idx = lax.broadcasted_iota(jnp.int32, (c, c), 1)
    tri_lo = (s_idx <= t_idx).astype(BF16)
    tri_up = (s_idx >= t_idx).astype(BF16)
    x = t_idx ^ s_idx
    lvl = jnp.ones((c, c), jnp.int32)
    w = BASE_BLOCK
    while w < c:
        lvl = lvl + (x >= w).astype(jnp.int32)
        w *= 2
    code = jnp.where(t_idx > s_idx, lvl, jnp.where(t_idx < s_idx, -lvl, 0))
    return tri_lo, tri_up, code


def _scan_kernel(n_x, n_c, v_ref, q_ref, kf_ref, kb_ref, gf_ref, gb_ref, ga_ref, hn_ref,
                 tlo_ref, tup_ref, code_ref, ya_ref, o_s, qt_s, u_s, d_s, st_s):
    c = SCAN_CHUNK
    hd = HEAD_DIM
    n = n_x + n_c
    tri_lo = tlo_ref[...]
    tri_up = tup_ref[...]

    def rows_of(i):
        return slice(i * c, (i + 1) * c)

    def gate_stage(hh, i):
        rows = rows_of(i)
        return dict(i=i, rows=rows,
                    kf=kf_ref[0, hh, rows, :].astype(F32), kb=kb_ref[0, hh, rows, :].astype(F32),
                    q=q_ref[0, hh, rows, :].astype(F32),
                    bcf=_chunk_cumsum(tri_lo, gf_ref, hh, rows), bcb=_chunk_cumsum(tri_up, gb_ref, hh, rows))

    def decay_stage(hh, s):
        q, kf, kb, bcf, bcb, rows = s["q"], s["kf"], s["kb"], s["bcf"], s["bcb"], s["rows"]
        s["prods"] = _level_products(q, kf, bcf, kb, bcb)
        qt_s[hh, rows, :hd] = (q * jnp.exp2(bcf)).astype(BF16)
        qt_s[hh, rows, hd:] = (q * jnp.exp2(bcb)).astype(BF16)
        ftot = bcf[c - 1:c, :]
        btot = bcb[0:1, :]
        khat = jnp.concatenate([kf * jnp.exp2(ftot - bcf), kb * jnp.exp2(btot - bcb)], axis=1)
        u_s[hh, s["i"]] = _dot_tn(v_ref[0, hh, rows, :], khat.astype(BF16))
        d_s[hh, s["i"]] = jnp.broadcast_to(jnp.concatenate([jnp.exp2(ftot), jnp.exp2(btot)], axis=1),
                                           (BASE_BLOCK, 2 * hd))

    def mix_stage(hh, s):
        p = _assemble_scores(s["prods"], code_ref)
        o_s[hh, s["rows"], :] = _dot(p.astype(BF16), v_ref[0, hh, s["rows"], :])

    for hh in range(SCAN_HEADS):
        staged = {}
        for k in range(-STAGE_SKEW, n + STAGE_SKEW):
            if 0 <= k < n:
                decay_stage(hh, staged[k])
            if 0 <= k + STAGE_SKEW < n:
                staged[k + STAGE_SKEW] = gate_stage(hh, k + STAGE_SKEW)
            if 0 <= k - STAGE_SKEW < n:
                mix_stage(hh, staged.pop(k - STAGE_SKEW))

    ctx_chunks = list(range(n_x, n))
    x_chunks = list(range(n_x))
    for hh in range(SCAN_HEADS):
        state = jnp.zeros((hd, hd), F32)
        for i in ctx_chunks + x_chunks:
            st_s[hh, i, :, :hd] = state.astype(BF16)
            state = state * d_s[hh, i, 0:1, :hd] + u_s[hh, i, :, :hd]
        state = jnp.zeros((hd, hd), F32)
        for i in ctx_chunks[::-1] + x_chunks[::-1]:
            st_s[hh, i, :, hd:] = state.astype(BF16)
            state = state * d_s[hh, i, 0:1, hd:] + u_s[hh, i, :, hd:]

    for hh in range(SCAN_HEADS):
        hn = hn_ref[hh]
        inter = {}
        for k in range(-STAGE_SKEW, n):
            if k + STAGE_SKEW < n:
                i = k + STAGE_SKEW
                inter[i] = _dot_nt(qt_s[hh, rows_of(i), :], st_s[hh, i])
            if k >= 0:
                rows = rows_of(k)
                o = o_s[hh, rows, :] + inter.pop(k)
                ms = jnp.mean(o * o, axis=-1, keepdims=True)
                y = o * lax.rsqrt(ms + EPS) * hn
                ya_ref[0, rows, hh * hd:(hh + 1) * hd] = (y * ga_ref[0, hh, rows, :].astype(F32)).astype(BF16)


def _scan(layer, p, hn_w, n_ctx_rows, consts):
    b, _, total, _ = p.shape
    depth, hgrn_width = hn_w.shape
    heads = hgrn_width // HEAD_DIM
    hs = SCAN_HEADS
    assert heads % hs == 0
    c = SCAN_CHUNK
    n_chunks = total // c
    n_c = n_ctx_rows // c

    def slab(group):
        return pl.BlockSpec((1, hs, total, HEAD_DIM), lambda bi, h: (bi, h + group * heads // hs, 0, 0))

    def slab_pair(group):
        return pl.BlockSpec((1, 2 * hs, total, HEAD_DIM), lambda bi, h: (bi, h + group * heads // (2 * hs), 0, 0))

    const = pl.BlockSpec((c, c), lambda bi, h: (0, 0))
    return pl.pallas_call(
        functools.partial(_scan_kernel, n_chunks - n_c, n_c),
        grid=(b, heads // hs),
        in_specs=[slab(G_V), slab(G_Q), slab(G_KF), slab(G_KB), slab_pair(G_GF), slab_pair(G_GB),
                  slab(G_GA), pl.BlockSpec((hs, 1, HEAD_DIM), lambda bi, h: (layer * heads // hs + h, 0, 0)),
                  const, const, const],
        out_specs=pl.BlockSpec((1, total, hs * HEAD_DIM), lambda bi, h: (bi, 0, h)),
        out_shape=jax.ShapeDtypeStruct((b, total, heads * HEAD_DIM), BF16),
        scratch_shapes=[
            pltpu.VMEM((hs, total, HEAD_DIM), F32),
            pltpu.VMEM((hs, total, 2 * HEAD_DIM), BF16),
            pltpu.VMEM((hs, n_chunks, HEAD_DIM, 2 * HEAD_DIM), F32),
            pltpu.VMEM((hs, n_chunks, BASE_BLOCK, 2 * HEAD_DIM), F32),
            pltpu.VMEM((hs, n_chunks, HEAD_DIM, 2 * HEAD_DIM), BF16),
        ],
        compiler_params=pltpu.CompilerParams(
            dimension_semantics=("parallel", "parallel"), vmem_limit_bytes=VMEM_LIMIT),
        name="scan",
    )(p, p, p, p, p, p, p, hn_w.reshape(depth * heads, 1, HEAD_DIM), *consts)


def _fourier_out_kernel(length, add_pos, final_norm, *refs):
    refs = list(refs)
    u_ref, g_ref, ab_ref, bias_ref, cs_ref, ya_ref, w_ref, x_ref, gate_ref = refs[:9]
    rest = refs[9:]
    pos_ref = rest.pop(0) if add_pos else None
    fnw_ref = rest.pop(0) if final_norm else None
    o_ref, tt_s = rest
    cg = GROUP_DIM
    slabs = cg // LANES
    half = length // 2

    @pl.when(pl.program_id(1) == 0)
    def _():
        l_idx = lax.broadcasted_iota(jnp.int32, (half, half), 0)
        m_idx = lax.broadcasted_iota(jnp.int32, (half, half), 1)
        rev = ((l_idx + m_idx == half) & (l_idx >= 1)).astype(BF16)
        def upper(g):
            return jnp.concatenate([u_ref[0, slabs * g + s, half:, :] for s in range(slabs)], axis=1)

        mirrored = [_dot(rev, upper(g)) for g in range(FOURIER_GROUPS)]
        for g in range(FOURIER_GROUPS):
            cols = slice(g * cg, (g + 1) * cg)
            lower = jnp.concatenate([u_ref[0, slabs * g + s, :half, :] for s in range(slabs)], axis=1)
            lower = lower.astype(F32)
            tt_s[:half, cols] = _dot((lower + mirrored[g]).astype(BF16), ab_ref[0, g, :cg, :]).astype(BF16)
            tt_s[half:, cols] = _dot((lower - mirrored[g]).astype(BF16), ab_ref[0, g, cg:, :]).astype(BF16)
            tt_s[half:half + 1, cols] = _dot(upper(g)[:16], ab_ref[0, g, :cg, :])[0:1].astype(BF16)

    acc = _dot(cs_ref[...], tt_s[...])
    gate_f = jnp.concatenate([g_ref[0, j] for j in range(acc.shape[1] // LANES)], axis=1)
    y_f = ((acc + bias_ref[0]) * gate_f.astype(F32)).astype(BF16)
    wa = ya_ref.shape[2]
    out = _dot(ya_ref[0], w_ref[0, :wa, :]) + _dot(y_f, w_ref[0, wa:, :])
    x = x_ref[0]
    if add_pos:
        x = x + pos_ref[...]
    x = x + gate_ref[0] * out
    if final_norm:
        ms = jnp.mean(x * x, axis=-1, keepdims=True)
        x = x * lax.rsqrt(ms + EPS) * fnw_ref[...]
    o_ref[0] = x


def _fourier_out(layer, p, y_a, row0, ab, bias, cs, w_bf16, x, gate, pos, final_norm_w):
    b, length, d = x.shape
    depth, groups, _, cg = ab.shape
    width = groups * cg
    wa = y_a.shape[2]
    slabs = width // LANES
    tm = min(512, length)
    assert row0 % length == 0
    u_block = G_U * HEADS_PER_GROUP // slabs
    gate_block = G_GATE_F * HEADS_PER_GROUP // slabs
    add_pos = pos is not None
    final_norm = final_norm_w is not None
    in_specs = [
        pl.BlockSpec((1, slabs, length, LANES), lambda bi, i: (bi, u_block, row0 // length, 0)),
        pl.BlockSpec((1, slabs, tm, LANES), lambda bi, i: (bi, gate_block, row0 // tm + i, 0)),
        pl.BlockSpec((1, groups, 2 * cg, cg), lambda bi, i: (layer, 0, 0, 0)),
        pl.BlockSpec((1, 1, width), lambda bi, i: (layer, 0, 0)),
        pl.BlockSpec((tm, length), lambda bi, i: (i, 0)),
        pl.BlockSpec((1, tm, wa), lambda bi, i: (bi, row0 // tm + i, 0)),
        pl.BlockSpec((1, wa + width, d), lambda bi, i: (layer, 0, 0)),
        pl.BlockSpec((1, tm, d), lambda bi, i: (bi, i, 0)),
        pl.BlockSpec((1, 1, d), lambda bi, i: (bi, 0, 0)),
    ]
    args = [p, p, ab, bias.reshape(depth, 1, width), cs, y_a, w_bf16, x, gate.reshape(b, 1, d)]
    if add_pos:
        in_specs.append(pl.BlockSpec((tm, d), lambda bi, i: (i, 0)))
        args.append(pos)
    if final_norm:
        in_specs.append(pl.BlockSpec((1, d), lambda bi, i: (0, 0)))
        args.append(final_norm_w.reshape(1, d))
    return pl.pallas_call(
        functools.partial(_fourier_out_kernel, length, add_pos, final_norm),
        grid=(b, length // tm),
        in_specs=in_specs,
        out_specs=pl.BlockSpec((1, tm, d), lambda bi, i: (bi, i, 0)),
        out_shape=jax.ShapeDtypeStruct((b, length, d), F32),
        scratch_shapes=[pltpu.VMEM((length, width), BF16)],
        compiler_params=pltpu.CompilerParams(
            dimension_semantics=("parallel", "arbitrary"), vmem_limit_bytes=VMEM_LIMIT),
        name="fourier_out",
    )(*args)


def _seq_dft_matrix(length):
    half = length // 2
    col = jnp.arange(length, dtype=jnp.int32)[None, :]
    freq = jnp.where(col <= half, col, col - half)
    shift = jnp.where(col <= half, length // 4, half)
    a = jnp.arange(length // GRID_W, dtype=jnp.int32)[:, None]
    r = jnp.arange(GRID_W, dtype=jnp.int32)[:, None]
    step = 2.0 * math.pi / length
    alpha = ((GRID_W * a * freq) % length).astype(F32) * step
    beta = ((r * freq + shift) % length).astype(F32) * step
    table = (jnp.sin(alpha)[:, None, :] * jnp.cos(beta)[None, :, :]
             + jnp.cos(alpha)[:, None, :] * jnp.sin(beta)[None, :, :])
    return (table.reshape(length, length) * (1.0 / math.sqrt(length * GROUP_DIM))).astype(BF16)


def kernel(x, c, ctx, c_ctx, norm_w, w_ada, b_ada, w_in, lower_bounds, hgrn_norm_w,
           w_fourier, b_fourier, w_out, final_norm_w):
    b, length, d = x.shape
    depth = w_in.shape[0]
    ctx_len = ctx.shape[1]
    assert hgrn_norm_w.shape[1] == HEADS_PER_GROUP * HEAD_DIM

    pos = _sincos_2d(length, d)
    lbs = _lower_bounds(lower_bounds)
    rows = ((b + 1 + 7) // 8) * 8
    c_rows = jnp.zeros((rows, d), F32).at[:b].set(c).at[b].set(c_ctx)
    mod = _ada_all(c_rows, w_ada, b_ada)
    ab = _fold_fourier_weights(w_fourier)
    cs_x = _seq_dft_matrix(length)
    cs_c = _seq_dft_matrix(ctx_len)
    w_in_b = w_in.astype(BF16)
    w_out_b = w_out.astype(BF16)
    consts = _scan_constants()

    def both_streams(l, k):
        part = mod[l, :, k * d:(k + 1) * d]
        return jnp.concatenate([part[:b], jnp.broadcast_to(part[b:b + 1], (b, d))], axis=0)

    for l in range(depth):
        last = l == depth - 1
        shift, scale, gate = (both_streams(l, k) for k in range(3))
        p = _inproj(l, x, ctx, pos if l == 0 else None, norm_w, scale, shift, lbs, w_in_b)
        y_a = _scan(l, p, hgrn_norm_w, ctx_len, consts)
        x_next = _fourier_out(l, p, y_a, 0, ab, b_fourier, cs_x, w_out_b, x, gate[:b],
                              pos if l == 0 else None, final_norm_w if last else None)
        if not last:
            ctx = _fourier_out(l, p, y_a, length, ab, b_fourier, cs_c, w_out_b, ctx, gate[b:], None, None)
        x = x_next
    return x
```
